```python
import math
import jax, jax.numpy as jnp
from jax import lax
import numpy as np

D_MODEL = 2048
BATCH = 1
SEQ = 8192
DEPTH = 2
DEC_BATCH = 128
DEC_SEQ = 4
PAST_LEN = 2048
PAGE_SIZE = 128

N_A_LAYERS = DEPTH // 2
N_B_LAYERS = DEPTH - N_A_LAYERS
N_DENSE = (DEPTH + 1) // 2
N_MOE = DEPTH // 2

GDN_HEADS = 16
GDN_DK = 128
GDN_DV = 128
CONV_W = 4
CHUNK = 64
GDN_QK = GDN_HEADS * GDN_DK
GDN_VD = GDN_HEADS * GDN_DV
CONV_DIM = 2 * GDN_QK + GDN_VD
GDN_PROJ = CONV_DIM + GDN_VD + 2 * GDN_HEADS

ATTN_HEADS = 16
HEAD_DIM = 128
ATTN_DIM = ATTN_HEADS * HEAD_DIM
MOBA_BLOCK = 256
MOBA_TOPK = 3
Q_BLOCK = 64
ROPE_THETA = 10000.0

D_FF = 7168
N_EXPERTS = 8
TOP_K = 2
D_FF_EXPERT = 7168
EPS = 1e-6

kernel_name = 'yoco_gdn_moba_hybrid_step'


def rms_norm(x, g):
    xf = x.astype(jnp.float32)
    y = xf * lax.rsqrt(jnp.mean(xf * xf, axis=-1, keepdims=True) + EPS)
    return (y * g.astype(jnp.float32)).astype(x.dtype)


def l2_normalize(x):
    return x * lax.rsqrt(jnp.sum(x * x, axis=-1, keepdims=True) + EPS)


def rope(x, pos):
    half = HEAD_DIM // 2
    inv = ROPE_THETA ** (-jnp.arange(half, dtype=jnp.float32) / half)
    ang = pos.astype(jnp.float32)[:, None] * inv[None, :]
    cos = jnp.cos(ang)[:, None, :]
    sin = jnp.sin(ang)[:, None, :]
    xf = x.astype(jnp.float32)
    x1, x2 = xf[..., :half], xf[..., half:]
    return jnp.concatenate([x1 * cos - x2 * sin, x2 * cos + x1 * sin], axis=-1).astype(x.dtype)


def swiglu(h, w_gu, w_down):
    g, u = jnp.split(h @ w_gu, 2, axis=-1)
    return (jax.nn.silu(g) * u) @ w_down


def moe_swiglu(h, w_router, w_gu, w_down):
    logits = (h @ w_router).astype(jnp.float32)
    top_v, top_i = lax.top_k(logits, TOP_K)
    top_w = jax.nn.softmax(top_v, axis=-1)
    gates = jnp.sum(jax.nn.one_hot(top_i, N_EXPERTS, dtype=jnp.float32) * top_w[..., None], axis=-2)
    out = jnp.zeros(h.shape, jnp.float32)
    for e in range(N_EXPERTS):
        out = out + gates[..., e:e + 1] * swiglu(h, w_gu[e], w_down[e]).astype(jnp.float32)
    return out.astype(h.dtype)


def gated_delta_rule(q, k, v, g, beta, s0):
    B, T, H, DK = q.shape
    DV = v.shape[-1]
    pad = (-T) % CHUNK

    def padt(a):
        return jnp.pad(a, [(0, 0), (0, pad)] + [(0, 0)] * (a.ndim - 2))

    q, k, v, g, beta = padt(q), padt(k), padt(v), padt(g), padt(beta)
    N = (T + pad) // CHUNK

    def blocks(a):
        a = a.reshape((B, N, CHUNK) + a.shape[2:])
        return jnp.moveaxis(a, (1, 3), (0, 2))

    qc, kc, vc = blocks(q), blocks(k), blocks(v)
    gc = jnp.cumsum(blocks(g), axis=-1)
    bc = blocks(beta)
    kb = kc * bc[..., None]
    vb = vc * bc[..., None]
    idx = jnp.arange(CHUNK)
    lower = idx[:, None] >= idx[None, :]
    strict = idx[:, None] > idx[None, :]
    decay = jnp.exp(jnp.where(lower, gc[..., :, None] - gc[..., None, :], -jnp.inf))
    kk = jnp.einsum('nbhid,nbhjd->nbhij', kb, kc) * decay
    a_mat = jnp.where(strict, kk, 0.0) + jnp.eye(CHUNK, dtype=jnp.float32)
    rhs = jnp.concatenate([vb, kb * jnp.exp(gc)[..., None]], axis=-1)
    sol = lax.linalg.triangular_solve(a_mat, rhs, left_side=True, lower=True, unit_diagonal=True)
    u, w = sol[..., :DV], sol[..., DV:]
    qk = jnp.einsum('nbhid,nbhjd->nbhij', qc, kc) * decay

    def step(S, xs):
        q_n, k_n, u_n, w_n, g_n, qk_n = xs
        v_new = u_n - jnp.einsum('bhcd,bhde->bhce', w_n, S)
        o = (jnp.einsum('bhcd,bhde->bhce', q_n * jnp.exp(g_n)[..., None], S)
             + jnp.einsum('bhij,bhje->bhie', qk_n, v_new))
        g_last = g_n[..., -1]
        S = (S * jnp.exp(g_last)[..., None, None]
             + jnp.einsum('bhcd,bhce->bhde', k_n * jnp.exp(g_last[..., None] - g_n)[..., None], v_new))
        return S, o

    S, o = lax.scan(step, s0, (qc, kc, u, w, gc, qk))
    o = jnp.moveaxis(o, (0, 2), (1, 3)).reshape(B, N * CHUNK, H, DV)[:, :T]
    return o, S


def gated_deltanet(h, w_in, conv_w, a_log, dt_bias, norm_g, w_out, s0, conv_buf):
    B, T, _ = h.shape
    proj = h @ w_in
    qkv = proj[..., :CONV_DIM]
    z = proj[..., CONV_DIM:CONV_DIM + GDN_VD].reshape(B, T, GDN_HEADS, GDN_DV)
    b_raw = proj[..., CONV_DIM + GDN_VD:CONV_DIM + GDN_VD + GDN_HEADS]
    a_raw = proj[..., CONV_DIM + GDN_VD + GDN_HEADS:]
    xc = jnp.concatenate([conv_buf.astype(qkv.dtype), qkv], axis=1)
    acc = xc[:, 0:T] * conv_w[0]
    for j in range(1, CONV_W):
        acc = acc + xc[:, j:j + T] * conv_w[j]
    act = jax.nn.silu(acc).astype(jnp.float32)
    new_buf = xc[:, T:]
    q = l2_normalize(act[..., :GDN_QK].reshape(B, T, GDN_HEADS, GDN_DK)) * (GDN_DK ** -0.5)
    k = l2_normalize(act[..., GDN_QK:2 * GDN_QK].reshape(B, T, GDN_HEADS, GDN_DK))
    v = act[..., 2 * GDN_QK:].reshape(B, T, GDN_HEADS, GDN_DV)
    beta = jax.nn.sigmoid(b_raw.astype(jnp.float32))
    g = -jnp.exp(a_log.astype(jnp.float32)) * jax.nn.softplus(a_raw.astype(jnp.float32) + dt_bias.astype(jnp.float32))
    o, s_new = gated_delta_rule(q, k, v, g, beta, s0.astype(jnp.float32))
    o = rms_norm(o, norm_g) * jax.nn.silu(z.astype(jnp.float32))
    y = o.reshape(B, T, GDN_VD).astype(h.dtype) @ w_out
    return y, s_new.astype(s0.dtype), new_buf.astype(conv_buf.dtype)


def shared_kv(x, cs, pos, kv_ada_w, kv_ada_b, kv_norm_g, w_kv):
    B, T, _ = x.shape
    sh, sc = jnp.split((cs @ kv_ada_w + kv_ada_b)[:, None, :], 2, axis=-1)
    hk = rms_norm(x, kv_norm_g) * (1 + sc) + sh
    k, v = jnp.split(hk @ w_kv, 2, axis=-1)
    k = rope(k.reshape(B, T, ATTN_HEADS, HEAD_DIM), pos)
    v = v.reshape(B, T, ATTN_HEADS, HEAD_DIM)
    return k, v


def to_blocks(a, nb):
    a = jnp.pad(a, [(0, nb * MOBA_BLOCK - a.shape[0]), (0, 0), (0, 0)])
    return a.reshape(nb, MOBA_BLOCK, ATTN_HEADS, HEAD_DIM).transpose(2, 0, 1, 3)


def moba_group(q, kb, vb, kmean, p0):
    Tq = q.shape[0]
    nb = kb.shape[1]
    ob = p0 // MOBA_BLOCK
    qpos = p0 + jnp.arange(Tq)
    qf = q.astype(jnp.float32)
    gate = jnp.einsum('qhd,hnd->hqn', qf, kmean)
    gate = jnp.where(jnp.arange(nb) < ob, gate, -jnp.inf)
    _, sel = lax.top_k(gate, MOBA_TOPK)
    sel_valid = sel < ob
    h_idx = jnp.arange(ATTN_HEADS)[:, None]
    sel_flat = sel.reshape(ATTN_HEADS, Tq * MOBA_TOPK)
    ksel = kb[h_idx, sel_flat].reshape(ATTN_HEADS, Tq, MOBA_TOPK * MOBA_BLOCK, HEAD_DIM)
    vsel = vb[h_idx, sel_flat].reshape(ATTN_HEADS, Tq, MOBA_TOPK * MOBA_BLOCK, HEAD_DIM)
    scale = HEAD_DIM ** -0.5
    s_sel = jnp.einsum('qhd,hqkd->hqk', qf, ksel.astype(jnp.float32)) * scale
    s_sel = jnp.where(jnp.repeat(sel_valid, MOBA_BLOCK, axis=-1), s_sel, -jnp.inf)
    k_own = lax.dynamic_index_in_dim(kb, ob, axis=1, keepdims=False)
    v_own = lax.dynamic_index_in_dim(vb, ob, axis=1, keepdims=False)
    s_own = jnp.einsum('qhd,hsd->hqs', qf, k_own.astype(jnp.float32)) * scale
    kpos = ob * MOBA_BLOCK + jnp.arange(MOBA_BLOCK)
    s_own = jnp.where(kpos[None, None, :] <= qpos[None, :, None], s_own, -jnp.inf)
    p = jax.nn.softmax(jnp.concatenate([s_sel, s_own], axis=-1), axis=-1)
    n_sel = MOBA_TOPK * MOBA_BLOCK
    o = (jnp.einsum('hqk,hqkd->qhd', p[..., :n_sel], vsel.astype(jnp.float32))
         + jnp.einsum('hqs,hsd->qhd', p[..., n_sel:], v_own.astype(jnp.float32)))
    return o.astype(q.dtype)


def moba_prompt(q, k, v):
    B, T = q.shape[0], q.shape[1]
    nb = max(-(-T // MOBA_BLOCK), MOBA_TOPK)
    kb = jax.vmap(lambda a: to_blocks(a, nb))(k)
    vb = jax.vmap(lambda a: to_blocks(a, nb))(v)
    kmean = jnp.mean(kb.astype(jnp.float32), axis=3)
    nc = T // Q_BLOCK
    qc = q.reshape(B, nc, Q_BLOCK, ATTN_HEADS, HEAD_DIM)

    def one(i):
        b = i // nc
        c = i % nc
        return moba_group(qc[b, c], kb[b], vb[b], kmean[b], c * Q_BLOCK)

    out = lax.map(one, jnp.arange(B * nc))
    return out.reshape(B, T, ATTN_HEADS, HEAD_DIM)


def moba_sample(q, k_new, v_new, cache_k, cache_v, page_table):
    Tq = q.shape[1]
    past = page_table.shape[1] * PAGE_SIZE
    nb = max(-(-(past + Tq) // MOBA_BLOCK), MOBA_TOPK)

    def one(args):
        qb, kn, vn, pages = args
        k_all = jnp.concatenate([cache_k[pages].reshape(past, ATTN_HEADS, HEAD_DIM).astype(kn.dtype), kn], axis=0)
        v_all = jnp.concatenate([cache_v[pages].reshape(past, ATTN_HEADS, HEAD_DIM).astype(vn.dtype), vn], axis=0)
        kb = to_blocks(k_all, nb)
        vb = to_blocks(v_all, nb)
        kmean = jnp.mean(kb.astype(jnp.float32), axis=2)
        return moba_group(qb, kb, vb, kmean, past)

    return lax.map(one, (q, k_new, v_new, page_table))


def _trunk(p, x, c, pos, delta0, conv0, attend):
    B, T, _ = x.shape
    cs = jax.nn.silu(c)
    deltas, convs = [], []
    k_sh = None
    v_sh = None
    for l in range(DEPTH):
        mod = (cs @ p['ada_w'][l] + p['ada_b'][l])[:, None, :]
        sh_m, sc_m, gt_m, sh_f, sc_f, gt_f = jnp.split(mod, 6, axis=-1)
        h = rms_norm(x, p['ln_mix_pre'][l]) * (1 + sc_m) + sh_m
        if l < N_A_LAYERS:
            y, s_new, buf_new = gated_deltanet(h, p['a_w_in'][l], p['a_conv_w'][l], p['a_A_log'][l], p['a_dt_bias'][l],
                                               p['a_norm_g'][l], p['a_w_out'][l], delta0[:, l], conv0[:, l])
            deltas.append(s_new)
            convs.append(buf_new)
        else:
            j = l - N_A_LAYERS
            q = rope((h @ p['b_w_q'][j]).reshape(B, T, ATTN_HEADS, HEAD_DIM), pos)
            o = attend(q, k_sh, v_sh)
            y = o.reshape(B, T, ATTN_DIM) @ p['b_w_out'][j]
        x = x + gt_m * rms_norm(y, p['ln_mix_post'][l])
        h = rms_norm(x, p['ln_ffn_pre'][l]) * (1 + sc_f) + sh_f
        if l % 2 == 0:
            y = swiglu(h, p['f_w_gu'][l // 2], p['f_w_down'][l // 2])
        else:
            y = moe_swiglu(h, p['m_router'][l // 2], p['m_w_gu'][l // 2], p['m_w_down'][l // 2])
        x = x + gt_f * rms_norm(y, p['ln_ffn_post'][l])
        if l == N_A_LAYERS - 1:
            k_sh, v_sh = shared_kv(x, cs, pos, p['kv_ada_w'], p['kv_ada_b'], p['kv_norm_g'], p['w_kv'])
    return x, jnp.stack(deltas, axis=1), jnp.stack(convs, axis=1), k_sh, v_sh


def setup_inputs(seed: int = 0) -> dict:
    key = jax.random.key(seed)
    ks = iter(jax.random.split(key, 40))

    def nrm(shape, s):
        return jax.random.normal(next(ks), shape, jnp.float32) * s

    D = D_MODEL
    n_pages = PAST_LEN // PAGE_SIZE
    n_used = DEC_BATCH * n_pages
    n_phys = n_used + max(1, n_used // 4)
    x_prompt = nrm((BATCH, SEQ, D), 1.0)
    x_sample = nrm((DEC_BATCH, DEC_SEQ, D), 1.0)
    cache_k = nrm((n_phys, PAGE_SIZE, ATTN_HEADS, HEAD_DIM), 1.0)
    cache_v = nrm((n_phys, PAGE_SIZE, ATTN_HEADS, HEAD_DIM), 1.0)
    perm = jax.random.permutation(next(ks), n_phys)
    page_table = perm[:n_used].reshape(DEC_BATCH, n_pages).astype(jnp.int32)
    state_delta = nrm((DEC_BATCH, N_A_LAYERS, GDN_HEADS, GDN_DK, GDN_DV), 0.1)
    state_conv = nrm((DEC_BATCH, N_A_LAYERS, CONV_W - 1, CONV_DIM), 1.0)
    c_prompt = nrm((BATCH, D), 1.0)
    c_sample = nrm((DEC_BATCH, D), 1.0)
    ada_w = nrm((DEPTH, D, 6 * D), 0.5 * D ** -0.5)
    ada_b = nrm((DEPTH, 6 * D), 0.01)
    ln_mix_pre = 1.0 + nrm((DEPTH, D), 0.05)
    ln_mix_post = 1.0 + nrm((DEPTH, D), 0.05)
    ln_ffn_pre = 1.0 + nrm((DEPTH, D), 0.05)
    ln_ffn_post = 1.0 + nrm((DEPTH, D), 0.05)
    a_w_in = nrm((N_A_LAYERS, D, GDN_PROJ), D ** -0.5)
    a_conv_w = nrm((N_A_LAYERS, CONV_W, CONV_DIM), 0.5)
    a_A_log = jnp.log(jax.random.uniform(next(ks), (N_A_LAYERS, GDN_HEADS), jnp.float32, 1.0, 16.0))
    dt = jnp.exp(jax.random.uniform(next(ks), (N_A_LAYERS, GDN_HEADS), jnp.float32, math.log(1e-3), math.log(0.1)))
    a_dt_bias = dt + jnp.log(-jnp.expm1(-dt))
    a_norm_g = 1.0 + nrm((N_A_LAYERS, GDN_DV), 0.05)
    a_w_out = nrm((N_A_LAYERS, GDN_VD, D), GDN_VD ** -0.5)
    kv_ada_w = nrm((D, 2 * D), 0.5 * D ** -0.5)
    kv_ada_b = nrm((2 * D,), 0.01)
    kv_norm_g = 1.0 + nrm((D,), 0.05)
    w_kv = nrm((D, 2 * ATTN_DIM), D ** -0.5)
    b_w_q = nrm((N_B_LAYERS, D, ATTN_DIM), D ** -0.5)
    b_w_out = nrm((N_B_LAYERS, ATTN_DIM, D), ATTN_DIM ** -0.5)
    f_w_gu = nrm((N_DENSE, D, 2 * D_FF), D ** -0.5)
    f_w_down = nrm((N_DENSE, D_FF, D), D_FF ** -0.5)
    m_router = nrm((N_MOE, D, N_EXPERTS), D ** -0.5)
    m_w_gu = nrm((N_MOE, N_EXPERTS, D, 2 * D_FF_EXPERT), D ** -0.5)
    m_w_down = nrm((N_MOE, N_EXPERTS, D_FF_EXPERT, D), D_FF_EXPERT ** -0.5)
    return {'x_prompt': x_prompt, 'x_sample': x_sample, 'cache_k': cache_k, 'cache_v': cache_v,
            'page_table': page_table, 'state_delta': state_delta, 'state_conv': state_conv,
            'c_prompt': c_prompt, 'c_sample': c_sample, 'ada_w': ada_w, 'ada_b': ada_b,
            'ln_mix_pre': ln_mix_pre, 'ln_mix_post': ln_mix_post, 'ln_ffn_pre': ln_ffn_pre, 'ln_ffn_post': ln_ffn_post,
            'a_w_in': a_w_in, 'a_conv_w': a_conv_w, 'a_A_log': a_A_log, 'a_dt_bias': a_dt_bias,
            'a_norm_g': a_norm_g, 'a_w_out': a_w_out, 'kv_ada_w': kv_ada_w, 'kv_ada_b': kv_ada_b,
            'kv_norm_g': kv_norm_g, 'w_kv': w_kv, 'b_w_q': b_w_q, 'b_w_out': b_w_out,
            'f_w_gu': f_w_gu, 'f_w_down': f_w_down, 'm_router': m_router, 'm_w_gu': m_w_gu, 'm_w_down': m_w_down}


def reference(x_prompt, x_sample, cache_k, cache_v, page_table, state_delta, state_conv, c_prompt, c_sample,
              ada_w, ada_b, ln_mix_pre, ln_mix_post, ln_ffn_pre, ln_ffn_post,
              a_w_in, a_conv_w, a_A_log, a_dt_bias, a_norm_g, a_w_out,
              kv_ada_w, kv_ada_b, kv_norm_g, w_kv, b_w_q, b_w_out,
              f_w_gu, f_w_down, m_router, m_w_gu, m_w_down):
    p = {'ada_w': ada_w, 'ada_b': ada_b, 'ln_mix_pre': ln_mix_pre, 'ln_mix_post': ln_mix_post,
         'ln_ffn_pre': ln_ffn_pre, 'ln_ffn_post': ln_ffn_post, 'a_w_in': a_w_in, 'a_conv_w': a_conv_w,
         'a_A_log': a_A_log, 'a_dt_bias': a_dt_bias, 'a_norm_g': a_norm_g, 'a_w_out': a_w_out,
         'kv_ada_w': kv_ada_w, 'kv_ada_b': kv_ada_b, 'kv_norm_g': kv_norm_g, 'w_kv': w_kv,
         'b_w_q': b_w_q, 'b_w_out': b_w_out, 'f_w_gu': f_w_gu, 'f_w_down': f_w_down,
         'm_router': m_router, 'm_w_gu': m_w_gu, 'm_w_down': m_w_down}
    B, T, _ = x_prompt.shape
    TS = x_sample.shape[1]
    past = page_table.shape[1] * PAGE_SIZE
    pos_p = jnp.arange(T)
    pos_s = past + jnp.arange(TS)
    delta0 = jnp.zeros((B, N_A_LAYERS, GDN_HEADS, GDN_DK, GDN_DV), state_delta.dtype)
    conv0 = jnp.zeros((B, N_A_LAYERS, CONV_W - 1, CONV_DIM), state_conv.dtype)
    y_prompt, delta_p, conv_p, k_p, v_p = _trunk(p, x_prompt, c_prompt, pos_p, delta0, conv0, moba_prompt)

    def attend_sample(q, k, v):
        return moba_sample(q, k, v, cache_k, cache_v, page_table)

    y_sample, delta_s, conv_s, k_s, v_s = _trunk(p, x_sample, c_sample, pos_s, state_delta, state_conv, attend_sample)
    return (y_prompt, y_sample, delta_p, conv_p, k_p, v_p, delta_s, conv_s, k_s, v_s)
```

```python
import functools
import math

import jax
import jax.numpy as jnp
from jax import lax
from jax.experimental import pallas as pl
from jax.experimental.pallas import tpu as pltpu

F32 = jnp.float32
BF16 = jnp.bfloat16
NEG_INF = float("-inf")

EPS = 1e-6
HEAD = 128
GDN_CHUNK = 64
CONV_W = 4
MOBA_BLOCK = 256
MOBA_TOPK = 3
ROPE_THETA = 10000.0
MOE_TOPK = 2
LANES = 128
VMEM_LIMIT = 52 * 1024 * 1024


def _params(*sem):
    return pltpu.CompilerParams(dimension_semantics=sem, vmem_limit_bytes=VMEM_LIMIT)


def _tile(n, pref, mult=8):
    if n <= pref:
        return n
    t = (pref // mult) * mult
    while t >= mult:
        if n % t == 0:
            return t
        t -= mult
    return n


def _dot(a, b):
    return jnp.dot(a, b, preferred_element_type=F32)


def _dot_nt(a, b):
    return lax.dot_general(a, b, (((1,), (1,)), ((), ())), preferred_element_type=F32)


def _dot_tn(a, b):
    return lax.dot_general(a, b, (((0,), (0,)), ((), ())), preferred_element_type=F32)


def _split(a):
    hi = a.astype(BF16)
    lo = (a - hi.astype(F32)).astype(BF16)
    return hi, lo


def _dot3(a, b, dot=_dot):
    ah, al = _split(a)
    bh, bl = _split(b)
    return dot(ah, bh) + (dot(ah, bl) + dot(al, bh))


def _dot_ones(ones, a, left):
    hi, mid = _split(a)
    lo = (a - hi.astype(F32) - mid.astype(F32)).astype(BF16)
    if left:
        return _dot(ones, hi) + (_dot(ones, mid) + _dot(ones, lo))
    return _dot(hi, ones) + (_dot(mid, ones) + _dot(lo, ones))


def _sigmoid(x):
    return 1.0 / (1.0 + jnp.exp(-x))


def _silu(x):
    return x * _sigmoid(x)


def _softplus(x):
    return jnp.maximum(x, 0.0) + jnp.log(1.0 + jnp.exp(-jnp.abs(x)))


def _rms(x, g):
    return x * lax.rsqrt(jnp.mean(x * x, axis=-1, keepdims=True) + EPS) * g


def _norm_mod(x, g, sc, sh):
    return _rms(x, g) * (1.0 + sc) + sh


def _mod_spec(mod, tm, k):
    if mod.shape[0] == 1:
        return pl.BlockSpec((1, k), lambda i, *_: (0, 0))
    return pl.BlockSpec((tm, k), lambda i, *_: (i, 0))


def _mod_kernel(c_ref, w_ref, b_ref, o_ref):
    o_ref[...] = _dot3(_silu(c_ref[...]), w_ref[...]) + b_ref[...]


def _modulation(c, w, b):
    r, d = c.shape
    n = w.shape[1]
    tn = _tile(n, 512, LANES)
    return pl.pallas_call(
        _mod_kernel,
        name="adaln_modulation",
        out_shape=jax.ShapeDtypeStruct((r, n), F32),
        grid=(n // tn,),
        in_specs=[pl.BlockSpec((r, d), lambda j: (0, 0)),
                  pl.BlockSpec((d, tn), lambda j: (0, j)),
                  pl.BlockSpec((1, tn), lambda j: (0, j))],
        out_specs=pl.BlockSpec((r, tn), lambda j: (0, j)),
        compiler_params=_params("parallel"),
    )(c, w, b.reshape(1, n))


def _rope_tile(y, cos, sin):
    outs = []
    for h in range(y.shape[1] // HEAD):
        yh = y[:, h * HEAD:(h + 1) * HEAD]
        outs.append(yh * cos + pltpu.roll(yh, HEAD // 2, 1) * sin)
    return outs[0] if len(outs) == 1 else jnp.concatenate(outs, axis=1)


def _nm_mm_kernel(x_ref, g_ref, sc_ref, sh_ref, w_ref, cos_ref, sin_ref, o_ref, h_scr, *, n_rope):
    j = pl.program_id(1)

    @pl.when(j == 0)
    def _():
        h_scr[...] = _norm_mod(x_ref[...], g_ref[...], sc_ref[...], sh_ref[...]).astype(BF16)

    y = _dot(h_scr[...], w_ref[...])
    if n_rope == 0:
        o_ref[...] = y
    else:
        @pl.when(j < n_rope)
        def _():
            o_ref[...] = _rope_tile(y, cos_ref[...], sin_ref[...])

        @pl.when(j >= n_rope)
        def _():
            o_ref[...] = y


def _norm_mod_matmul(x, g, sc, sh, w, cos=None, sin=None, rope_cols=0, tm_pref=1024, tn_pref=512):
    m, k = x.shape
    n = w.shape[1]
    tm = _tile(m, tm_pref)
    tn = _tile(math.gcd(n, rope_cols) if rope_cols else n, tn_pref, LANES)
    if cos is None:
        cos = jnp.zeros((m, HEAD), F32)
        sin = cos
    assert rope_cols % tn == 0
    return pl.pallas_call(
        functools.partial(_nm_mm_kernel, n_rope=rope_cols // tn),
        name="norm_mod_matmul",
        out_shape=jax.ShapeDtypeStruct((m, n), F32),
        grid=(m // tm, n // tn),
        in_specs=[pl.BlockSpec((tm, k), lambda i, j: (i, 0)),
                  pl.BlockSpec((1, k), lambda i, j: (0, 0)),
                  _mod_spec(sc, tm, k), _mod_spec(sh, tm, k),
                  pl.BlockSpec((k, tn), lambda i, j: (0, j)),
                  pl.BlockSpec((tm, HEAD), lambda i, j: (i, 0)),
                  pl.BlockSpec((tm, HEAD), lambda i, j: (i, 0))],
        out_specs=pl.BlockSpec((tm, tn), lambda i, j: (i, j)),
        scratch_shapes=[pltpu.VMEM((tm, k), BF16)],
        compiler_params=_params("parallel", "arbitrary"),
    )(x, g.reshape(1, k), sc, sh, w, cos, sin)


def _gates_kernel(x_ref, g_ref, sc_ref, sh_ref, w_ref, alog_ref, dtb_ref, o_ref, *, heads):
    h = _norm_mod(x_ref[...], g_ref[...], sc_ref[...], sh_ref[...])
    ba = _dot3(h, w_ref[...])
    lane = lax.broadcasted_iota(jnp.int32, ba.shape, 1)
    decay = -jnp.exp(alog_ref[...]) * _softplus(ba + dtb_ref[...])
    o_ref[...] = jnp.where(lane < heads, _sigmoid(ba), decay)


def _gdn_gates(x, g, sc, sh, w_ba, a_log, dt_bias):
    m, k = x.shape
    heads = a_log.shape[0]
    tm = _tile(m, 512)
    pad = jnp.zeros((heads,), F32)
    alog2 = jnp.concatenate([pad, a_log]).reshape(1, 2 * heads)
    dtb2 = jnp.concatenate([pad, dt_bias]).reshape(1, 2 * heads)
    return pl.pallas_call(
        functools.partial(_gates_kernel, heads=heads),
        name="gdn_gates",
        out_shape=jax.ShapeDtypeStruct((m, 2 * heads), F32),
        grid=(m // tm,),
        in_specs=[pl.BlockSpec((tm, k), lambda i: (i, 0)),
                  pl.BlockSpec((1, k), lambda i: (0, 0)),
                  _mod_spec(sc, tm, k), _mod_spec(sh, tm, k),
                  pl.BlockSpec((k, 2 * heads), lambda i: (0, 0)),
                  pl.BlockSpec((1, 2 * heads), lambda i: (0, 0)),
                  pl.BlockSpec((1, 2 * heads), lambda i: (0, 0))],
        out_specs=pl.BlockSpec((tm, 2 * heads), lambda i: (i, 0)),
        compiler_params=_params("parallel"),
    )(x, g.reshape(1, k), sc, sh, w_ba, alog2, dtb2)


def _qkv_act(acc, part):
    act = _silu(acc)
    outs = []
    for h in range(act.shape[1] // HEAD):
        a = act[:, h * HEAD:(h + 1) * HEAD]
        inv = lax.rsqrt(jnp.sum(a * a, axis=-1, keepdims=True) + EPS)
        fac = jnp.where(part == 0, inv * (HEAD ** -0.5), jnp.where(part == 1, inv, 1.0))
        outs.append(a * fac)
    return jnp.concatenate(outs, axis=1)


def _conv_seq_kernel(x_ref, w_ref, b0_ref, o_ref, xs, *, tt):
    part = pl.program_id(0)
    t = pl.program_id(1)

    @pl.when(t == 0)
    def _():
        xs[0:8, :] = b0_ref[...]

    @pl.when(t > 0)
    def _():
        xs[0:8, :] = xs[tt:tt + 8, :]

    xs[8:8 + tt, :] = x_ref[...]
    w = w_ref[...]
    acc = xs[5:5 + tt, :] * w[0:1, :]
    for j in range(1, CONV_W):
        acc = acc + xs[5 + j:5 + j + tt, :] * w[j:j + 1, :]
    o_ref[...] = _qkv_act(acc, part)


def _gdn_conv_seq(qkvz, conv_w, buf, width):
    t_len = qkvz.shape[0]
    tt = _tile(t_len, 512)
    b0 = jnp.concatenate([jnp.zeros((8 - (CONV_W - 1), 3 * width), F32), buf], axis=0)
    return pl.pallas_call(
        functools.partial(_conv_seq_kernel, tt=tt),
        name="gdn_conv_seq",
        out_shape=jax.ShapeDtypeStruct((t_len, 3 * width), F32),
        grid=(3, t_len // tt),
        in_specs=[pl.BlockSpec((tt, width), lambda c, t: (t, c)),
                  pl.BlockSpec((CONV_W, width), lambda c, t: (0, c)),
                  pl.BlockSpec((8, width), lambda c, t: (0, c))],
        out_specs=pl.BlockSpec((tt, width), lambda c, t: (t, c)),
        scratch_shapes=[pltpu.VMEM((tt + 8, width), F32)],
        compiler_params=_params("parallel", "arbitrary"),
    )(qkvz, conv_w, b0)


def _conv_step_kernel(x_ref, w_ref, o_ref, *, steps):
    part = pl.program_id(0)
    w = w_ref[...]
    for t in range(steps):
        acc = x_ref[t] * w[0:1, :]
        for j in range(1, CONV_W):
            acc = acc + x_ref[t + j] * w[j:j + 1, :]
        o_ref[t] = _qkv_act(acc, part)


def _gdn_conv_steps(xc, conv_w, width):
    rows, b, _ = xc.shape
    steps = rows - (CONV_W - 1)
    return pl.pallas_call(
        functools.partial(_conv_step_kernel, steps=steps),
        name="gdn_conv_steps",
        out_shape=jax.ShapeDtypeStruct((steps, b, 3 * width), F32),
        grid=(3,),
        in_specs=[pl.BlockSpec((rows, b, width), lambda c: (0, 0, c)),
                  pl.BlockSpec((CONV_W, width), lambda c: (0, c))],
        out_specs=pl.BlockSpec((steps, b, width), lambda c: (0, 0, c)),
        compiler_params=_params("parallel"),
    )(xc, conv_w)


def _unit_lower_inverse(a, c):
    row = lax.broadcasted_iota(jnp.int32, (c, c), 0)
    col = lax.broadcasted_iota(jnp.int32, (c, c), 1)
    eye = (row == col).astype(F32)
    base = min(c, 16)
    same = (row // base) == (col // base)
    d = jnp.where(same, a, 0.0)
    t = eye - d
    x = d
    for _ in range(int(math.log2(base)) - 1):
        x = _dot3(x, x)
        t = t + _dot3(t, x)
    size = base
    while size < c:
        inner = same
        size *= 2
        same = (row // size) == (col // size)
        off = jnp.where(jnp.logical_and(same, jnp.logical_not(inner)), a, 0.0)
        t = t - _dot3(t, _dot3(off, t))
    return t


def _gdn_chunk_kernel(q_ref, k_ref, v_ref, bg_ref, gt_ref, s0_ref, o_ref, s_out_ref, s_scr, *, heads, c):
    n = pl.program_id(0)

    @pl.when(n == 0)
    def _():
        s_scr[...] = s0_ref[...]

    row = lax.broadcasted_iota(jnp.int32, (c, c), 0)
    col = lax.broadcasted_iota(jnp.int32, (c, c), 1)
    lower = row >= col
    strict = row > col
    bg = bg_ref[...]
    gcum = _dot_ones(lower.astype(BF16), bg, left=True)
    gcum_t = _dot_ones((row <= col).astype(BF16), gt_ref[0], left=False)

    for h in range(heads):
        sl = slice(h * HEAD, (h + 1) * HEAD)
        q = q_ref[:, sl]
        k = k_ref[:, sl]
        v = v_ref[:, sl]
        beta = bg[:, h:h + 1]
        gi = gcum[:, heads + h:heads + h + 1]
        gj = gcum_t[h:h + 1, :]
        decay = jnp.exp(jnp.where(lower, gi - gj, NEG_INF))
        kb = k * beta
        vb = v * beta
        egi = jnp.exp(gi)
        qk_kk = _dot3(jnp.concatenate([q, kb], axis=0), k, _dot_nt)
        qk = qk_kk[:c] * decay
        a = jnp.where(strict, qk_kk[c:] * decay, 0.0)
        t = _unit_lower_inverse(a, c)
        sol = _dot3(t, jnp.concatenate([vb, kb * egi], axis=1))
        u = sol[:, :HEAD]
        w = sol[:, HEAD:]
        s = s_scr[h]
        ws_qs = _dot3(jnp.concatenate([w, q * egi], axis=0), s)
        v_new = u - ws_qs[:c]
        o_ref[:, sl] = ws_qs[c:] + _dot3(qk, v_new)
        g_last = gcum[c - 1:c, heads + h:heads + h + 1]
        kd = k * jnp.exp(g_last - gi)
        s_scr[h] = s * jnp.exp(g_last) + _dot3(kd, v_new, _dot_tn)

    @pl.when(n == pl.num_programs(0) - 1)
    def _():
        s_out_ref[...] = s_scr[...]


def _gdn_chunked(act, bg, s0, heads):
    t_len = act.shape[0]
    c = GDN_CHUNK
    assert t_len % c == 0
    width = heads * HEAD
    n_chunks = t_len // c
    g_t = bg[:, heads:].reshape(n_chunks, c, heads).transpose(0, 2, 1)
    return pl.pallas_call(
        functools.partial(_gdn_chunk_kernel, heads=heads, c=c),
        name="gdn_chunked",
        out_shape=(jax.ShapeDtypeStruct((t_len, width), F32),
                   jax.ShapeDtypeStruct((heads, HEAD, HEAD), F32)),
        grid=(n_chunks,),
        in_specs=[pl.BlockSpec((c, width), lambda n: (n, 0)),
                  pl.BlockSpec((c, width), lambda n: (n, 1)),
                  pl.BlockSpec((c, width), lambda n: (n, 2)),
                  pl.BlockSpec((c, 2 * heads), lambda n: (n, 0)),
                  pl.BlockSpec((1, heads, c), lambda n: (n, 0, 0)),
                  pl.BlockSpec((heads, HEAD, HEAD), lambda n: (0, 0, 0))],
        out_specs=(pl.BlockSpec((c, width), lambda n: (n, 0)),
                   pl.BlockSpec((heads, HEAD, HEAD), lambda n: (0, 0, 0))),
        scratch_shapes=[pltpu.VMEM((heads, HEAD, HEAD), F32)],
        compiler_params=_params("arbitrary"),
    )(act, act, act, bg, g_t, s0)


def _gdn_step_kernel(beta_ref, g_ref, qt_ref, kt_ref, v_ref, s0_ref, o_ref, s_ref, *, seqs, steps, heads):
    bo = pl.program_id(0)
    h = pl.program_id(1)
    for bl in range(seqs):
        s = s0_ref[bl]
        for t in range(steps):
            colx = bl * steps + t
            idx = ((bo * seqs + bl) * steps + t) * heads + h
            beta = beta_ref[idx]
            a = jnp.exp(jnp.full((1, HEAD), g_ref[idx], F32))
            kc = jnp.broadcast_to(kt_ref[:, colx:colx + 1], (HEAD, HEAD))
            qc = jnp.broadcast_to(qt_ref[:, colx:colx + 1], (HEAD, HEAD))
            r = jnp.sum(kc * s, axis=0, keepdims=True)
            u = beta * (v_ref[colx:colx + 1, :] - a * r)
            s = a * s + kc * u
            o_ref[colx:colx + 1, :] = jnp.sum(qc * s, axis=0, keepdims=True)
        s_ref[bl] = s


def _gdn_steps(act_tm, bg, s0, heads, seqs=8):
    steps, b, _ = act_tm.shape
    assert b % seqs == 0
    nbo = b // seqs
    width = heads * HEAD
    lanes = seqs * steps

    def cols(a):
        a = a.reshape(steps, nbo, seqs, heads, HEAD)
        return a.transpose(1, 3, 4, 2, 0).reshape(nbo, heads, HEAD, lanes)

    qt = cols(act_tm[..., :width])
    kt = cols(act_tm[..., width:2 * width])
    v = act_tm[..., 2 * width:].reshape(steps, nbo, seqs, heads, HEAD)
    v = v.transpose(1, 3, 2, 0, 4).reshape(nbo, heads, lanes, HEAD)
    beta = bg[:, :heads].reshape(-1)
    g = bg[:, heads:].reshape(-1)
    smem = pl.BlockSpec(memory_space=pltpu.SMEM)
    o, s_new = pl.pallas_call(
        functools.partial(_gdn_step_kernel, seqs=seqs, steps=steps, heads=heads),
        name="gdn_steps",
        out_shape=(jax.ShapeDtypeStruct((nbo, heads, lanes, HEAD), F32),
                   jax.ShapeDtypeStruct(s0.shape, F32)),
        grid=(nbo, heads),
        in_specs=[smem, smem,
                  pl.BlockSpec((None, None, HEAD, lanes), lambda i, h: (i, h, 0, 0)),
                  pl.BlockSpec((None, None, HEAD, lanes), lambda i, h: (i, h, 0, 0)),
                  pl.BlockSpec((None, None, lanes, HEAD), lambda i, h: (i, h, 0, 0)),
                  pl.BlockSpec((seqs, None, HEAD, HEAD), lambda i, h: (i, h, 0, 0))],
        out_specs=(pl.BlockSpec((None, None, lanes, HEAD), lambda i, h: (i, h, 0, 0)),
                   pl.BlockSpec((seqs, None, HEAD, HEAD), lambda i, h: (i, h, 0, 0))),
        compiler_params=_params("parallel", "parallel"),
    )(beta, g, qt, kt, v, s0)
    o = o.reshape(nbo, heads, seqs, steps, HEAD).transpose(0, 2, 3, 1, 4).reshape(b * steps, width)
    return o, s_new


def _gdn_out_gate(o, z, g):
    outs = []
    for h in range(o.shape[1] // HEAD):
        sl = slice(h * HEAD, (h + 1) * HEAD)
        outs.append(_rms(o[:, sl], g) * _silu(z[:, sl]))
    return jnp.concatenate(outs, axis=1)


def _mm_post_kernel(*refs, gated):
    if gated:
        a_ref, z_ref, ng_ref, w_ref, x_ref, gt_ref, pg_ref, o_ref = refs
        a = _gdn_out_gate(a_ref[...], z_ref[...], ng_ref[...]).astype(BF16)
    else:
        a_ref, w_ref, x_ref, gt_ref, pg_ref, o_ref = refs
        a = a_ref[...].astype(BF16)
    y = _dot(a, w_ref[...])
    o_ref[...] = x_ref[...] + gt_ref[...] * _rms(y, pg_ref[...])


def _matmul_post(a, w, x, gt, post_g, z=None, z_col=0, norm_g=None):
    m, k = a.shape
    d = w.shape[1]
    tm = _tile(m, 256)
    gated = z is not None
    in_specs = [pl.BlockSpec((tm, k), lambda i: (i, 0))]
    args = [a]
    if gated:
        in_specs += [pl.BlockSpec((tm, k), lambda i: (i, z_col)), pl.BlockSpec((1, HEAD), lambda i: (0, 0))]
        args += [z, norm_g.reshape(1, HEAD)]
    in_specs += [pl.BlockSpec((k, d), lambda i: (0, 0)),
                 pl.BlockSpec((tm, d), lambda i: (i, 0)),
                 _mod_spec(gt, tm, d),
                 pl.BlockSpec((1, d), lambda i: (0, 0))]
    args += [w, x, gt, post_g.reshape(1, d)]
    return pl.pallas_call(
        functools.partial(_mm_post_kernel, gated=gated),
        name="matmul_post",
        out_shape=jax.ShapeDtypeStruct((m, d), F32),
        grid=(m // tm,),
        in_specs=in_specs,
        out_specs=pl.BlockSpec((tm, d), lambda i: (i, 0)),
        compiler_params=_params("parallel"),
    )(*args)


def _ffn_kernel(x_ref, g_ref, sc_ref, sh_ref, wg_ref, wu_ref, wd_ref, gt_ref, pg_ref, o_ref, h_scr, acc):
    f = pl.program_id(1)

    @pl.when(f == 0)
    def _():
        h_scr[...] = _norm_mod(x_ref[...], g_ref[...], sc_ref[...], sh_ref[...]).astype(BF16)
        acc[...] = jnp.zeros_like(acc)

    h = h_scr[...]
    act = (_silu(_dot(h, wg_ref[...])) * _dot(h, wu_ref[...])).astype(BF16)
    acc[...] += _dot(act, wd_ref[...])

    @pl.when(f == pl.num_programs(1) - 1)
    def _():
        o_ref[...] = x_ref[...] + gt_ref[...] * _rms(acc[...], pg_ref[...])


def _dense_ffn(x, g, sc, sh, w_gu, w_down, gt, post_g, tm_pref=512, tf_pref=512):
    m, d = x.shape
    ff = w_down.shape[0]
    tm = _tile(m, tm_pref)
    tf = _tile(ff, tf_pref, LANES)
    nf = ff // tf
    return pl.pallas_call(
        _ffn_kernel,
        name="dense_ffn",
        out_shape=jax.ShapeDtypeStruct((m, d), F32),
        grid=(m // tm, nf),
        in_specs=[pl.BlockSpec((tm, d), lambda i, f: (i, 0)),
                  pl.BlockSpec((1, d), lambda i, f: (0, 0)),
                  _mod_spec(sc, tm, d), _mod_spec(sh, tm, d),
                  pl.BlockSpec((d, tf), lambda i, f: (0, f)),
                  pl.BlockSpec((d, tf), lambda i, f: (0, nf + f)),
                  pl.BlockSpec((tf, d), lambda i, f: (f, 0)),
                  _mod_spec(gt, tm, d),
                  pl.BlockSpec((1, d), lambda i, f: (0, 0))],
        out_specs=pl.BlockSpec((tm, d), lambda i, f: (i, 0)),
        scratch_shapes=[pltpu.VMEM((tm, d), BF16), pltpu.VMEM((tm, d), F32)],
        compiler_params=_params("parallel", "arbitrary"),
    )(x, g.reshape(1, d), sc, sh, w_gu, w_gu, w_down, gt, post_g.reshape(1, d))


def _router_kernel(x_ref, g_ref, sc_ref, sh_ref, wr_ref, h_ref, r_ref, *, experts):
    h = _norm_mod(x_ref[...], g_ref[...], sc_ref[...], sh_ref[...])
    h_ref[...] = h.astype(BF16)
    logits = _dot3(h, wr_ref[...])
    lane = lax.broadcasted_iota(jnp.int32, logits.shape, 1)
    lanef = lane.astype(F32)
    logits = jnp.where(lane < experts, logits, NEG_INF)
    m1 = jnp.max(logits, axis=-1, keepdims=True)
    i1 = jnp.min(jnp.where(logits == m1, lanef, float(LANES)), axis=-1, keepdims=True)
    rest = jnp.where(lanef == i1, NEG_INF, logits)
    m2 = jnp.max(rest, axis=-1, keepdims=True)
    i2 = jnp.min(jnp.where(rest == m2, lanef, float(LANES)), axis=-1, keepdims=True)
    e2 = jnp.exp(m2 - m1)
    w1 = 1.0 / (1.0 + e2)
    w2 = e2 / (1.0 + e2)
    r_ref[...] = jnp.where(lane == 0, i1, jnp.where(lane == 1, i2, jnp.where(lane == 2, w1, jnp.where(lane == 3, w2, 0.0))))


def _router(x, g, sc, sh, w_router):
    m, d = x.shape
    experts = w_router.shape[1]
    tm = _tile(m, 512)
    wr = jnp.pad(w_router, ((0, 0), (0, LANES - experts)))
    return pl.pallas_call(
        functools.partial(_router_kernel, experts=experts),
        name="moe_router",
        out_shape=(jax.ShapeDtypeStruct((m, d), BF16), jax.ShapeDtypeStruct((m, LANES), F32)),
        grid=(m // tm,),
        in_specs=[pl.BlockSpec((tm, d), lambda i: (i, 0)),
                  pl.BlockSpec((1, d), lambda i: (0, 0)),
                  _mod_spec(sc, tm, d), _mod_spec(sh, tm, d),
                  pl.BlockSpec((d, LANES), lambda i: (0, 0))],
        out_specs=(pl.BlockSpec((tm, d), lambda i: (i, 0)), pl.BlockSpec((tm, LANES), lambda i: (i, 0))),
        compiler_params=_params("parallel"),
    )(x, g.reshape(1, d), sc, sh, wr)


def _moe_ffn_kernel(te_ref, nu_ref, h_ref, wg_ref, wu_ref, wd_ref, o_ref, acc):
    i = pl.program_id(0)
    f = pl.program_id(1)
    last = pl.num_programs(1) - 1
    used = i < nu_ref[0]

    @pl.when(jnp.logical_and(used, f == 0))
    def _():
        acc[...] = jnp.zeros_like(acc)

    @pl.when(used)
    def _():
        h = h_ref[...]
        act = (_silu(_dot(h, wg_ref[...])) * _dot(h, wu_ref[...])).astype(BF16)
        acc[...] += _dot(act, wd_ref[...])

    @pl.when(jnp.logical_and(used, f == last))
    def _():
        o_ref[...] = acc[...]

    @pl.when(jnp.logical_and(jnp.logical_not(used), f == last))
    def _():
        o_ref[...] = jnp.zeros_like(o_ref)


def _moe_ffn(hs, tile_expert, n_used, w_gu, w_down, tm, tf_pref=512):
    mp, d = hs.shape
    ff = w_down.shape[1]
    tf = _tile(ff, tf_pref, LANES)
    nf = ff // tf

    def fidx(i, f, nu):
        return jnp.where(i < nu[0], f, nf - 1)

    grid_spec = pltpu.PrefetchScalarGridSpec(
        num_scalar_prefetch=2,
        grid=(mp // tm, nf),
        in_specs=[pl.BlockSpec((tm, d), lambda i, f, te, nu: (i, 0)),
                  pl.BlockSpec((None, d, tf), lambda i, f, te, nu: (te[i], 0, fidx(i, f, nu))),
                  pl.BlockSpec((None, d, tf), lambda i, f, te, nu: (te[i], 0, nf + fidx(i, f, nu))),
                  pl.BlockSpec((None, tf, d), lambda i, f, te, nu: (te[i], fidx(i, f, nu), 0))],
        out_specs=pl.BlockSpec((tm, d), lambda i, f, te, nu: (i, 0)),
        scratch_shapes=[pltpu.VMEM((tm, d), F32)],
    )
    return pl.pallas_call(
        _moe_ffn_kernel,
        name="moe_ffn",
        out_shape=jax.ShapeDtypeStruct((mp, d), F32),
        grid_spec=grid_spec,
        compiler_params=_params("arbitrary", "arbitrary"),
    )(tile_expert, n_used, hs, w_gu, w_gu, w_down)


def _combine_kernel(y1_ref, y2_ref, r_ref, x_ref, gt_ref, pg_ref, o_ref):
    r = r_ref[...]
    y = r[:, 2:3] * y1_ref[...] + r[:, 3:4] * y2_ref[...]
    o_ref[...] = x_ref[...] + gt_ref[...] * _rms(y, pg_ref[...])


def _moe_combine(y1, y2, route, x, gt, post_g):
    m, d = x.shape
    tm = _tile(m, 256)
    return pl.pallas_call(
        _combine_kernel,
        name="moe_combine",
        out_shape=jax.ShapeDtypeStruct((m, d), F32),
        grid=(m // tm,),
        in_specs=[pl.BlockSpec((tm, d), lambda i: (i, 0)),
                  pl.BlockSpec((tm, d), lambda i: (i, 0)),
                  pl.BlockSpec((tm, LANES), lambda i: (i, 0)),
                  pl.BlockSpec((tm, d), lambda i: (i, 0)),
                  _mod_spec(gt, tm, d),
                  pl.BlockSpec((1, d), lambda i: (0, 0))],
        out_specs=pl.BlockSpec((tm, d), lambda i: (i, 0)),
        compiler_params=_params("parallel"),
    )(y1, y2, route, x, gt, post_g.reshape(1, d))


def _moe_layer(x, g, sc, sh, gt, post_g, w_router, w_gu, w_down, tm=512):
    m, d = x.shape
    experts = w_router.shape[1]
    h, route = _router(x, g, sc, sh, w_router)
    ids = route[:, :MOE_TOPK].astype(jnp.int32).reshape(-1)
    n_assign = m * MOE_TOPK
    counts = jnp.zeros((experts,), jnp.int32).at[ids].add(1)
    padded = ((counts + tm - 1) // tm) * tm
    starts_p = jnp.cumsum(padded) - padded
    starts = jnp.cumsum(counts) - counts
    order = jnp.argsort(ids, stable=True)
    sorted_ids = ids[order]
    dest_sorted = starts_p[sorted_ids] + (jnp.arange(n_assign, dtype=jnp.int32) - starts[sorted_ids])
    n_tiles = -(-(n_assign + experts * (tm - 1)) // tm)
    mp = n_tiles * tm
    src_tok = jnp.zeros((mp,), jnp.int32).at[dest_sorted].set((order // MOE_TOPK).astype(jnp.int32))
    dest = jnp.zeros((n_assign,), jnp.int32).at[order].set(dest_sorted.astype(jnp.int32))
    tile_start = jnp.arange(n_tiles, dtype=jnp.int32) * tm
    ends_p = jnp.cumsum(padded)
    tile_expert = jnp.minimum(jnp.searchsorted(ends_p, tile_start, side="right"), experts - 1).astype(jnp.int32)
    n_used = (ends_p[-1] // tm).astype(jnp.int32).reshape(1)
    last_e = tile_expert[jnp.maximum(n_used[0] - 1, 0)]
    tile_expert = jnp.where(jnp.arange(n_tiles) < n_used[0], tile_expert, last_e)
    hs = jnp.take(h, src_tok, axis=0)
    ys = _moe_ffn(hs, tile_expert, n_used, w_gu, w_down, tm)
    dest = dest.reshape(m, MOE_TOPK)
    y1 = jnp.take(ys, dest[:, 0], axis=0)
    y2 = jnp.take(ys, dest[:, 1], axis=0)
    return _moe_combine(y1, y2, route, x, gt, post_g)


def _block_mean_kernel(k_ref, o_ref):
    o_ref[0] = jnp.mean(k_ref[...], axis=0, keepdims=True)


def _block_means(k):
    t_len, width = k.shape
    nb = t_len // MOBA_BLOCK
    out = pl.pallas_call(
        _block_mean_kernel,
        name="moba_block_means",
        out_shape=jax.ShapeDtypeStruct((nb, 1, width), F32),
        grid=(nb,),
        in_specs=[pl.BlockSpec((MOBA_BLOCK, width), lambda j: (j, 0))],
        out_specs=pl.BlockSpec((1, 1, width), lambda j: (j, 0, 0)),
        compiler_params=_params("parallel"),
    )(k)
    return out.reshape(nb, width)


def _moba_seq_kernel(q_ref, k_ref, v_ref, km_ref, o_ref, *, nb):
    bs = MOBA_BLOCK
    ob = pl.program_id(1)
    scale = HEAD ** -0.5
    q = q_ref[...]
    gate = _dot3(q, km_ref[...], _dot_nt)
    blk = lax.broadcasted_iota(jnp.int32, (bs, nb), 1)
    blkf = blk.astype(F32)
    gate = jnp.where(blk < ob, gate, NEG_INF)
    sel = jnp.zeros((bs, nb), F32)
    for _ in range(MOBA_TOPK):
        m = jnp.max(gate, axis=-1, keepdims=True)
        first = jnp.min(jnp.where(gate == m, blkf, float(nb)), axis=-1, keepdims=True)
        pick = jnp.logical_and(blkf == first, m > NEG_INF)
        sel = jnp.where(pick, 1.0, sel)
        gate = jnp.where(pick, NEG_INF, gate)

    qb = q.astype(BF16)
    own = pl.multiple_of(ob * bs, bs)
    s = _dot_nt(qb, k_ref[pl.ds(own, bs), :].astype(BF16)) * scale
    row = lax.broadcasted_iota(jnp.int32, (bs, bs), 0)
    col = lax.broadcasted_iota(jnp.int32, (bs, bs), 1)
    s = jnp.where(col <= row, s, NEG_INF)
    m0 = jnp.max(s, axis=-1, keepdims=True)
    p = jnp.exp(s - m0)
    l0 = jnp.sum(p, axis=-1, keepdims=True)
    acc0 = _dot(p.astype(BF16), v_ref[pl.ds(own, bs), :].astype(BF16))

    def body(j, carry):
        m_run, l_run, acc = carry
        start = pl.multiple_of(j * bs, bs)
        picked = jnp.sum(jnp.where(blk == j, sel, 0.0), axis=-1, keepdims=True) > 0.0
        sj = _dot_nt(qb, k_ref[pl.ds(start, bs), :].astype(BF16)) * scale
        sj = jnp.where(picked, sj, NEG_INF)
        m_new = jnp.maximum(m_run, jnp.max(sj, axis=-1, keepdims=True))
        alpha = jnp.exp(m_run - m_new)
        pj = jnp.exp(sj - m_new)
        l_new = alpha * l_run + jnp.sum(pj, axis=-1, keepdims=True)
        acc_new = alpha * acc + _dot(pj.astype(BF16), v_ref[pl.ds(start, bs), :].astype(BF16))
        return m_new, l_new, acc_new

    _, l_fin, acc_fin = lax.fori_loop(0, ob, body, (m0, l0, acc0))
    o_ref[...] = (acc_fin / l_fin).astype(o_ref.dtype)


def _moba_seq(q, k, v, heads):
    t_len, width = q.shape
    assert t_len % MOBA_BLOCK == 0
    nb = t_len // MOBA_BLOCK
    kmean = _block_means(k)
    return pl.pallas_call(
        functools.partial(_moba_seq_kernel, nb=nb),
        name="moba_seq",
        out_shape=jax.ShapeDtypeStruct((t_len, width), BF16),
        grid=(heads, nb),
        in_specs=[pl.BlockSpec((MOBA_BLOCK, HEAD), lambda h, i: (i, h)),
                  pl.BlockSpec((t_len, HEAD), lambda h, i: (0, h)),
                  pl.BlockSpec((t_len, HEAD), lambda h, i: (0, h)),
                  pl.BlockSpec((nb, HEAD), lambda h, i: (0, h))],
        out_specs=pl.BlockSpec((MOBA_BLOCK, HEAD), lambda h, i: (i, h)),
        compiler_params=_params("parallel", "arbitrary"),
    )(q, k, v, kmean)


def _moba_paged_kernel(pt_ref, q_ref, kn_ref, vn_ref, ka_ref, kb_ref, va_ref, vb_ref, o_ref,
                       qf, qb, acc, m_s, l_s, g_s, *, heads, steps, nblk, page):
    j = pl.program_id(1)
    rows = steps * heads
    width = heads * HEAD
    scale = HEAD ** -0.5
    sub = lax.broadcasted_iota(jnp.int32, (heads, width), 0)
    lane_head = lax.broadcasted_iota(jnp.int32, (heads, width), 1) // HEAD
    diag = sub == lane_head

    @pl.when(j == 0)
    def _():
        for t in range(steps):
            qrow = jnp.broadcast_to(q_ref[0, t:t + 1, :], (heads, width))
            qf[t * heads:(t + 1) * heads, :] = jnp.where(diag, qrow, 0.0)
        qb[...] = qf[...].astype(BF16)

    ka = ka_ref[0]
    kb = kb_ref[0]
    ksum = jnp.sum(ka, axis=0, keepdims=True) + jnp.sum(kb, axis=0, keepdims=True)
    gate = jnp.sum(qf[...] * ksum, axis=-1, keepdims=True) * (1.0 / (2 * page))
    g_s[j] = jnp.broadcast_to(gate, (rows, LANES))
    sa = _dot_nt(qb[...], ka.astype(BF16)) * scale
    sb = _dot_nt(qb[...], kb.astype(BF16)) * scale
    m = jnp.maximum(jnp.max(sa, axis=-1, keepdims=True), jnp.max(sb, axis=-1, keepdims=True))
    pa = jnp.exp(sa - m)
    pb = jnp.exp(sb - m)
    l = jnp.sum(pa, axis=-1, keepdims=True) + jnp.sum(pb, axis=-1, keepdims=True)
    acc[j] = _dot(pa.astype(BF16), va_ref[0].astype(BF16)) + _dot(pb.astype(BF16), vb_ref[0].astype(BF16))
    m_s[j] = jnp.broadcast_to(m, (rows, LANES))
    l_s[j] = jnp.broadcast_to(l, (rows, LANES))

    @pl.when(j == nblk - 1)
    def _():
        gates = [g_s[b] for b in range(nblk)]
        sel = []
        for b in range(nblk):
            rank = jnp.zeros((rows, LANES), F32)
            for o in range(nblk):
                if o == b:
                    continue
                ahead = gates[o] > gates[b]
                if o < b:
                    ahead = jnp.logical_or(ahead, gates[o] == gates[b])
                rank = rank + jnp.where(ahead, 1.0, 0.0)
            sel.append(rank < float(MOBA_TOPK))
        t_of_row = lax.broadcasted_iota(jnp.int32, (rows, LANES), 0) // heads
        s_own = []
        for s in range(steps):
            sc = jnp.sum(qf[...] * kn_ref[0, s:s + 1, :], axis=-1, keepdims=True) * scale
            s_own.append(jnp.where(t_of_row >= s, jnp.broadcast_to(sc, (rows, LANES)), NEG_INF))
        m_all = s_own[0]
        for s in range(1, steps):
            m_all = jnp.maximum(m_all, s_own[s])
        for b in range(nblk):
            m_all = jnp.maximum(m_all, jnp.where(sel[b], m_s[b], NEG_INF))
        den = jnp.zeros((rows, LANES), F32)
        out = jnp.zeros((rows, width), F32)
        for b in range(nblk):
            wb = jnp.where(sel[b], jnp.exp(m_s[b] - m_all), 0.0)
            den = den + wb * l_s[b]
            out = out + jnp.broadcast_to(wb[:, :1], (rows, width)) * acc[b]
        for s in range(steps):
            ps = jnp.exp(s_own[s] - m_all)
            den = den + ps
            out = out + jnp.broadcast_to(ps[:, :1], (rows, width)) * vn_ref[0, s:s + 1, :]
        out = out / jnp.broadcast_to(den[:, :1], (rows, width))
        for t in range(steps):
            o_ref[0, t:t + 1, :] = jnp.sum(jnp.where(diag, out[t * heads:(t + 1) * heads, :], 0.0),
                                           axis=0, keepdims=True)


def _moba_paged(q, k_new, v_new, cache_k, cache_v, page_table, heads):
    b, steps, width = q.shape
    n_phys, page = cache_k.shape[0], cache_k.shape[1]
    n_pages = page_table.shape[1]
    assert MOBA_BLOCK == 2 * page and n_pages % 2 == 0 and steps <= MOBA_BLOCK
    nblk = n_pages // 2
    rows = steps * heads
    ck = cache_k.reshape(n_phys, page, width)
    cv = cache_v.reshape(n_phys, page, width)
    pt = page_table.reshape(-1).astype(jnp.int32)

    def page_spec(off):
        return pl.BlockSpec((1, page, width), lambda i, j, pt: (pt[i * n_pages + 2 * j + off], 0, 0))

    new_spec = pl.BlockSpec((1, steps, width), lambda i, j, pt: (i, 0, 0))
    grid_spec = pltpu.PrefetchScalarGridSpec(
        num_scalar_prefetch=1,
        grid=(b, nblk),
        in_specs=[new_spec, new_spec, new_spec, page_spec(0), page_spec(1), page_spec(0), page_spec(1)],
        out_specs=pl.BlockSpec((1, steps, width), lambda i, j, pt: (i, 0, 0)),
        scratch_shapes=[pltpu.VMEM((rows, width), F32), pltpu.VMEM((rows, width), BF16),
                        pltpu.VMEM((nblk, rows, width), F32),
                        pltpu.VMEM((nblk, rows, LANES), F32), pltpu.VMEM((nblk, rows, LANES), F32),
                        pltpu.VMEM((nblk, rows, LANES), F32)],
    )
    return pl.pallas_call(
        functools.partial(_moba_paged_kernel, heads=heads, steps=steps, nblk=nblk, page=page),
        name="moba_paged",
        out_shape=jax.ShapeDtypeStruct((b, steps, width), F32),
        grid_spec=grid_spec,
        compiler_params=_params("parallel", "arbitrary"),
    )(pt, q, k_new, v_new, ck, ck, cv, cv)


def _rope_tables(pos):
    half = HEAD // 2
    inv = ROPE_THETA ** (-jnp.arange(half, dtype=F32) / half)
    ang = pos.astype(F32)[:, None] * inv[None, :]
    cos = jnp.cos(ang)
    sin = jnp.sin(ang)
    return jnp.concatenate([cos, cos], axis=1), jnp.concatenate([-sin, sin], axis=1)


def _chunks(mod, rep):
    parts = jnp.split(mod, 6, axis=-1)
    return [jnp.repeat(p, rep, axis=0) if rep > 1 else p for p in parts]


def kernel(x_prompt, x_sample, cache_k, cache_v, page_table, state_delta, state_conv, c_prompt, c_sample, ada_w, ada_b, ln_mix_pre, ln_mix_post, ln_ffn_pre, ln_ffn_post, a_w_in, a_conv_w, a_A_log, a_dt_bias, a_norm_g, a_w_out, kv_ada_w, kv_ada_b, kv_norm_g, w_kv, b_w_q, b_w_out, f_w_gu, f_w_down, m_router, m_w_gu, m_w_down):
    bp, t_p, d = x_prompt.shape
    bs, t_s, _ = x_sample.shape
    assert bp == 1
    gh = a_A_log.shape[1]
    ah = cache_k.shape[2]
    gw = gh * HEAD
    aw = ah * HEAD
    past = page_table.shape[1] * cache_k.shape[1]

    c_all = jnp.concatenate([c_prompt, c_sample], axis=0)
    r_all = c_all.shape[0]
    c_all = jnp.pad(c_all, ((0, (-r_all) % 8), (0, 0)))
    mods = [_modulation(c_all, ada_w[l], ada_b[l]) for l in range(2)]
    kv_mod = _modulation(c_all, kv_ada_w, kv_ada_b)

    def mod_p(m):
        return m[0:1]

    def mod_s(m):
        return m[1:1 + bs]

    xp = x_prompt.reshape(t_p, d)
    xs = x_sample.reshape(bs * t_s, d)

    w_in = a_w_in[0]
    n_conv = 3 * gw
    w_main = w_in[:, :n_conv + gw].astype(BF16)
    w_ba = w_in[:, n_conv + gw:]
    w_out0 = a_w_out[0].astype(BF16)

    sh_m, sc_m, gt_m, sh_f, sc_f, gt_f = _chunks(mod_p(mods[0]), 1)
    proj_p = _norm_mod_matmul(xp, ln_mix_pre[0], sc_m, sh_m, w_main)
    bg_p = _gdn_gates(xp, ln_mix_pre[0], sc_m, sh_m, w_ba, a_A_log[0], a_dt_bias[0])
    act_p = _gdn_conv_seq(proj_p, a_conv_w[0], jnp.zeros((CONV_W - 1, n_conv), F32), gw)
    o_p, delta_p = _gdn_chunked(act_p, bg_p, jnp.zeros((gh, HEAD, HEAD), F32), gh)
    conv_p = proj_p[t_p - (CONV_W - 1):, :n_conv]
    xp = _matmul_post(o_p, w_out0, xp, gt_m, ln_mix_post[0], z=proj_p, z_col=3, norm_g=a_norm_g[0])

    ssh_m, ssc_m, sgt_m, ssh_f, ssc_f, sgt_f = _chunks(mod_s(mods[0]), t_s)
    proj_s = _norm_mod_matmul(xs, ln_mix_pre[0], ssc_m, ssh_m, w_main)
    bg_s = _gdn_gates(xs, ln_mix_pre[0], ssc_m, ssh_m, w_ba, a_A_log[0], a_dt_bias[0])
    qkv_s = proj_s[:, :n_conv].reshape(bs, t_s, n_conv)
    xc_s = jnp.concatenate([state_conv[:, 0].transpose(1, 0, 2), qkv_s.transpose(1, 0, 2)], axis=0)
    act_s = _gdn_conv_steps(xc_s, a_conv_w[0], gw)
    o_s, delta_s = _gdn_steps(act_s, bg_s, state_delta[:, 0], gh)
    conv_s = qkv_s[:, t_s - (CONV_W - 1):]
    xs = _matmul_post(o_s, w_out0, xs, sgt_m, ln_mix_post[0], z=proj_s, z_col=3, norm_g=a_norm_g[0])

    w_gu0 = f_w_gu[0].astype(BF16)
    w_dn0 = f_w_down[0].astype(BF16)
    xp = _dense_ffn(xp, ln_ffn_pre[0], sc_f, sh_f, w_gu0, w_dn0, gt_f, ln_ffn_post[0])
    xs = _dense_ffn(xs, ln_ffn_pre[0], ssc_f, ssh_f, w_gu0, w_dn0, sgt_f, ln_ffn_post[0])

    w_kv_b = w_kv.astype(BF16)
    cos_p, sin_p = _rope_tables(jnp.arange(t_p))
    cos_s, sin_s = _rope_tables(jnp.tile(past + jnp.arange(t_s), bs))
    ksh, ksc = jnp.split(kv_mod, 2, axis=-1)
    kv_p = _norm_mod_matmul(xp, kv_norm_g, mod_p(ksc), mod_p(ksh), w_kv_b, cos_p, sin_p, rope_cols=aw)
    kv_s = _norm_mod_matmul(xs, kv_norm_g, jnp.repeat(mod_s(ksc), t_s, axis=0), jnp.repeat(mod_s(ksh), t_s, axis=0),
                            w_kv_b, cos_s, sin_s, rope_cols=aw)
    k_p, v_p = kv_p[:, :aw], kv_p[:, aw:]
    k_s, v_s = kv_s[:, :aw], kv_s[:, aw:]

    w_q = b_w_q[0].astype(BF16)
    w_out1 = b_w_out[0].astype(BF16)
    sh_m, sc_m, gt_m, sh_f, sc_f, gt_f = _chunks(mod_p(mods[1]), 1)
    ssh_m, ssc_m, sgt_m, ssh_f, ssc_f, sgt_f = _chunks(mod_s(mods[1]), t_s)
    q_p = _norm_mod_matmul(xp, ln_mix_pre[1], sc_m, sh_m, w_q, cos_p, sin_p, rope_cols=aw)
    q_s = _norm_mod_matmul(xs, ln_mix_pre[1], ssc_m, ssh_m, w_q, cos_s, sin_s, rope_cols=aw)
    att_p = _moba_seq(q_p, k_p, v_p, ah)
    att_s = _moba_paged(q_s.reshape(bs, t_s, aw), k_s.reshape(bs, t_s, aw), v_s.reshape(bs, t_s, aw),
                        cache_k, cache_v, page_table, ah)
    xp = _matmul_post(att_p, w_out1, xp, gt_m, ln_mix_post[1])
    xs = _matmul_post(att_s.reshape(bs * t_s, aw), w_out1, xs, sgt_m, ln_mix_post[1])

    x_all = jnp.concatenate([xp, xs], axis=0)

    def rows(p_mod, s_mod):
        return jnp.concatenate([jnp.broadcast_to(p_mod, (t_p, d)), s_mod], axis=0)

    y_all = _moe_layer(x_all, ln_ffn_pre[1], rows(sc_f, ssc_f), rows(sh_f, ssh_f), rows(gt_f, sgt_f),
                       ln_ffn_post[1], m_router[0], m_w_gu[0].astype(BF16), m_w_down[0].astype(BF16))
    y_p, y_s = y_all[:t_p], y_all[t_p:]

    return (y_p.reshape(1, t_p, d), y_s.reshape(bs, t_s, d),
            delta_p.reshape(1, 1, gh, HEAD, HEAD), conv_p.reshape(1, 1, CONV_W - 1, n_conv),
            k_p.reshape(1, t_p, ah, HEAD), v_p.reshape(1, t_p, ah, HEAD),
            delta_s.reshape(bs, 1, gh, HEAD, HEAD), conv_s.reshape(bs, 1, CONV_W - 1, n_conv),
            k_s.reshape(bs, t_s, ah, HEAD), v_s.reshape(bs, t_s, ah, HEAD))
```

```python
import functools
import math

import jax
import jax.numpy as jnp
from jax import lax
from jax.experimental import pallas as pl
from jax.experimental.pallas import tpu as pltpu

F32 = jnp.float32
BF16 = jnp.bfloat16
NEG_INF = float("-inf")
MASKED = -1e30

EPS = 1e-6
HEAD = 128
GDN_CHUNK = 64
CONV_W = 4
MOBA_BLOCK = 256
MOBA_TOPK = 3
ROPE_THETA = 10000.0
MOE_TOPK = 2
LANES = 128
VMEM_LIMIT = 52 * 1024 * 1024


def _params(*sem):
    return pltpu.CompilerParams(dimension_semantics=sem, vmem_limit_bytes=VMEM_LIMIT)


def _tile(n, pref, mult=8):
    if n <= pref:
        return n
    t = (pref // mult) * mult
    while t >= mult:
        if n % t == 0:
            return t
        t -= mult
    return n


def _dot(a, b):
    return jnp.dot(a, b, preferred_element_type=F32)


def _dot_nt(a, b):
    return lax.dot_general(a, b, (((1,), (1,)), ((), ())), preferred_element_type=F32)


def _dot_tn(a, b):
    return lax.dot_general(a, b, (((0,), (0,)), ((), ())), preferred_element_type=F32)


def _split(a):
    hi = a.astype(BF16)
    lo = (a - hi.astype(F32)).astype(BF16)
    return hi, lo


def _dot3(a, b, dot=_dot):
    ah, al = _split(a)
    bh, bl = _split(b)
    return dot(ah, bh) + (dot(ah, bl) + dot(al, bh))


def _dot_ones(ones, a, left):
    hi, mid = _split(a)
    lo = (a - hi.astype(F32) - mid.astype(F32)).astype(BF16)
    if left:
        return _dot(ones, hi) + (_dot(ones, mid) + _dot(ones, lo))
    return _dot(hi, ones) + (_dot(mid, ones) + _dot(lo, ones))


def _sigmoid(x):
    return 1.0 / (1.0 + jnp.exp(-x))


def _silu(x):
    return x * _sigmoid(x)


def _softplus(x):
    return jnp.maximum(x, 0.0) + jnp.log(1.0 + jnp.exp(-jnp.abs(x)))


def _rms(x, g):
    return x * lax.rsqrt(jnp.mean(x * x, axis=-1, keepdims=True) + EPS) * g


def _norm_mod(x, g, sc, sh):
    return _rms(x, g) * (1.0 + sc) + sh


def _mod_spec(mod, tm, k):
    if mod.shape[0] == 1:
        return pl.BlockSpec((1, k), lambda i, *_: (0, 0))
    return pl.BlockSpec((tm, k), lambda i, *_: (i, 0))


def _mod_kernel(c_ref, w_ref, b_ref, o_ref):
    o_ref[...] = _dot3(_silu(c_ref[...]), w_ref[...]) + b_ref[...]


def _modulation(c, w, b):
    r, d = c.shape
    n = w.shape[1]
    tn = _tile(n, 512, LANES)
    return pl.pallas_call(
        _mod_kernel,
        name="adaln_modulation",
        out_shape=jax.ShapeDtypeStruct((r, n), F32),
        grid=(n // tn,),
        in_specs=[pl.BlockSpec((r, d), lambda j: (0, 0)),
                  pl.BlockSpec((d, tn), lambda j: (0, j)),
                  pl.BlockSpec((1, tn), lambda j: (0, j))],
        out_specs=pl.BlockSpec((r, tn), lambda j: (0, j)),
        compiler_params=_params("parallel"),
    )(c, w, b.reshape(1, n))


def _rope_tile(y, cos, sin):
    outs = []
    for h in range(y.shape[1] // HEAD):
        yh = y[:, h * HEAD:(h + 1) * HEAD]
        outs.append(yh * cos + pltpu.roll(yh, HEAD // 2, 1) * sin)
    return outs[0] if len(outs) == 1 else jnp.concatenate(outs, axis=1)


def _nm_mm_kernel(x_ref, g_ref, sc_ref, sh_ref, w_ref, cos_ref, sin_ref, o_ref, h_scr, *, n_rope):
    j = pl.program_id(1)

    @pl.when(j == 0)
    def _():
        h_scr[...] = _norm_mod(x_ref[...], g_ref[...], sc_ref[...], sh_ref[...]).astype(BF16)

    y = _dot(h_scr[...], w_ref[...])
    if n_rope == 0:
        o_ref[...] = y
    else:
        @pl.when(j < n_rope)
        def _():
            o_ref[...] = _rope_tile(y, cos_ref[...], sin_ref[...])

        @pl.when(j >= n_rope)
        def _():
            o_ref[...] = y


def _norm_mod_matmul(x, g, sc, sh, w, cos=None, sin=None, rope_cols=0, tm_pref=1024, tn_pref=512):
    m, k = x.shape
    n = w.shape[1]
    tm = _tile(m, tm_pref)
    tn = _tile(math.gcd(n, rope_cols) if rope_cols else n, tn_pref, LANES)
    if cos is None:
        cos = jnp.zeros((m, HEAD), F32)
        sin = cos
    assert rope_cols % tn == 0
    return pl.pallas_call(
        functools.partial(_nm_mm_kernel, n_rope=rope_cols // tn),
        name="norm_mod_matmul",
        out_shape=jax.ShapeDtypeStruct((m, n), F32),
        grid=(m // tm, n // tn),
        in_specs=[pl.BlockSpec((tm, k), lambda i, j: (i, 0)),
                  pl.BlockSpec((1, k), lambda i, j: (0, 0)),
                  _mod_spec(sc, tm, k), _mod_spec(sh, tm, k),
                  pl.BlockSpec((k, tn), lambda i, j: (0, j)),
                  pl.BlockSpec((tm, HEAD), lambda i, j: (i, 0)),
                  pl.BlockSpec((tm, HEAD), lambda i, j: (i, 0))],
        out_specs=pl.BlockSpec((tm, tn), lambda i, j: (i, j)),
        scratch_shapes=[pltpu.VMEM((tm, k), BF16)],
        compiler_params=_params("parallel", "arbitrary"),
    )(x, g.reshape(1, k), sc, sh, w, cos, sin)


def _gates_kernel(x_ref, g_ref, sc_ref, sh_ref, w_ref, alog_ref, dtb_ref, o_ref, *, heads):
    h = _norm_mod(x_ref[...], g_ref[...], sc_ref[...], sh_ref[...])
    ba = _dot3(h, w_ref[...])
    lane = lax.broadcasted_iota(jnp.int32, ba.shape, 1)
    decay = -jnp.exp(alog_ref[...]) * _softplus(ba + dtb_ref[...])
    o_ref[...] = jnp.where(lane < heads, _sigmoid(ba), decay)


def _gdn_gates(x, g, sc, sh, w_ba, a_log, dt_bias):
    m, k = x.shape
    heads = a_log.shape[0]
    tm = _tile(m, 512)
    pad = jnp.zeros((heads,), F32)
    alog2 = jnp.concatenate([pad, a_log]).reshape(1, 2 * heads)
    dtb2 = jnp.concatenate([pad, dt_bias]).reshape(1, 2 * heads)
    return pl.pallas_call(
        functools.partial(_gates_kernel, heads=heads),
        name="gdn_gates",
        out_shape=jax.ShapeDtypeStruct((m, 2 * heads), F32),
        grid=(m // tm,),
        in_specs=[pl.BlockSpec((tm, k), lambda i: (i, 0)),
                  pl.BlockSpec((1, k), lambda i: (0, 0)),
                  _mod_spec(sc, tm, k), _mod_spec(sh, tm, k),
                  pl.BlockSpec((k, 2 * heads), lambda i: (0, 0)),
                  pl.BlockSpec((1, 2 * heads), lambda i: (0, 0)),
                  pl.BlockSpec((1, 2 * heads), lambda i: (0, 0))],
        out_specs=pl.BlockSpec((tm, 2 * heads), lambda i: (i, 0)),
        compiler_params=_params("parallel"),
    )(x, g.reshape(1, k), sc, sh, w_ba, alog2, dtb2)


def _qkv_act(acc, part):
    act = _silu(acc)
    outs = []
    for h in range(act.shape[1] // HEAD):
        a = act[:, h * HEAD:(h + 1) * HEAD]
        inv = lax.rsqrt(jnp.sum(a * a, axis=-1, keepdims=True) + EPS)
        fac = jnp.where(part == 0, inv * (HEAD ** -0.5), jnp.where(part == 1, inv, 1.0))
        outs.append(a * fac)
    return jnp.concatenate(outs, axis=1)


def _conv_seq_kernel(x_ref, w_ref, b0_ref, o_ref, xs, *, tt):
    part = pl.program_id(0)
    t = pl.program_id(1)

    @pl.when(t == 0)
    def _():
        xs[0:8, :] = b0_ref[...]

    @pl.when(t > 0)
    def _():
        xs[0:8, :] = xs[tt:tt + 8, :]

    xs[8:8 + tt, :] = x_ref[...]
    w = w_ref[...]
    acc = xs[5:5 + tt, :] * w[0:1, :]
    for j in range(1, CONV_W):
        acc = acc + xs[5 + j:5 + j + tt, :] * w[j:j + 1, :]
    o_ref[...] = _qkv_act(acc, part)


def _gdn_conv_seq(qkvz, conv_w, buf, width):
    t_len = qkvz.shape[0]
    tt = _tile(t_len, 512)
    b0 = jnp.concatenate([jnp.zeros((8 - (CONV_W - 1), 3 * width), F32), buf], axis=0)
    return pl.pallas_call(
        functools.partial(_conv_seq_kernel, tt=tt),
        name="gdn_conv_seq",
        out_shape=jax.ShapeDtypeStruct((t_len, 3 * width), F32),
        grid=(3, t_len // tt),
        in_specs=[pl.BlockSpec((tt, width), lambda c, t: (t, c)),
                  pl.BlockSpec((CONV_W, width), lambda c, t: (0, c)),
                  pl.BlockSpec((8, width), lambda c, t: (0, c))],
        out_specs=pl.BlockSpec((tt, width), lambda c, t: (t, c)),
        scratch_shapes=[pltpu.VMEM((tt + 8, width), F32)],
        compiler_params=_params("parallel", "arbitrary"),
    )(qkvz, conv_w, b0)


def _conv_step_kernel(x_ref, w_ref, o_ref, *, steps):
    part = pl.program_id(0)
    w = w_ref[...]
    for t in range(steps):
        acc = x_ref[t] * w[0:1, :]
        for j in range(1, CONV_W):
            acc = acc + x_ref[t + j] * w[j:j + 1, :]
        o_ref[t] = _qkv_act(acc, part)


def _gdn_conv_steps(xc, conv_w, width):
    rows, b, _ = xc.shape
    steps = rows - (CONV_W - 1)
    return pl.pallas_call(
        functools.partial(_conv_step_kernel, steps=steps),
        name="gdn_conv_steps",
        out_shape=jax.ShapeDtypeStruct((steps, b, 3 * width), F32),
        grid=(3,),
        in_specs=[pl.BlockSpec((rows, b, width), lambda c: (0, 0, c)),
                  pl.BlockSpec((CONV_W, width), lambda c: (0, c))],
        out_specs=pl.BlockSpec((steps, b, width), lambda c: (0, 0, c)),
        compiler_params=_params("parallel"),
    )(xc, conv_w)


def _unit_lower_inverses(mats, c):
    row = lax.broadcasted_iota(jnp.int32, (c, c), 0)
    col = lax.broadcasted_iota(jnp.int32, (c, c), 1)
    eye = (row == col).astype(F32)
    base = min(c, 16)
    same = (row // base) == (col // base)
    xs = [jnp.where(same, a, 0.0) for a in mats]
    ts = [eye - x for x in xs]
    for _ in range(int(math.log2(base)) - 1):
        xs = [_dot3(x, x) for x in xs]
        ts = [t + _dot3(t, x) for t, x in zip(ts, xs)]
    size = base
    while size < c:
        inner = same
        size *= 2
        same = (row // size) == (col // size)
        pick = jnp.logical_and(same, jnp.logical_not(inner))
        ys = [_dot3(jnp.where(pick, a, 0.0), t) for a, t in zip(mats, ts)]
        ts = [t - _dot3(t, y) for t, y in zip(ts, ys)]
    return ts


def _gdn_chunk_kernel(q_ref, k_ref, v_ref, bg_ref, gt_ref, s0_ref, o_ref, s_out_ref, s_scr, *, heads, c):
    n = pl.program_id(0)

    @pl.when(n == 0)
    def _():
        s_scr[...] = s0_ref[...]

    row = lax.broadcasted_iota(jnp.int32, (c, c), 0)
    col = lax.broadcasted_iota(jnp.int32, (c, c), 1)
    lower = row >= col
    strict = row > col
    bg = bg_ref[...]
    gcum = _dot_ones(lower.astype(BF16), bg, left=True)
    gcum_t = _dot_ones((row <= col).astype(BF16), gt_ref[0], left=False)

    hs = range(heads)
    sls = [slice(h * HEAD, (h + 1) * HEAD) for h in hs]
    q = [q_ref[:, sl] for sl in sls]
    k = [k_ref[:, sl] for sl in sls]
    gi = [gcum[:, heads + h:heads + h + 1] for h in hs]
    decay = [jnp.exp(jnp.where(lower, gi[h] - gcum_t[h:h + 1, :], NEG_INF)) for h in hs]
    kb = [k[h] * bg[:, h:h + 1] for h in hs]
    egi = [jnp.exp(g) for g in gi]
    qk_kk = [_dot3(jnp.concatenate([q[h], kb[h]], axis=0), k[h], _dot_nt) for h in hs]
    qk = [qk_kk[h][:c] * decay[h] for h in hs]
    ts = _unit_lower_inverses([jnp.where(strict, qk_kk[h][c:] * decay[h], 0.0) for h in hs], c)
    sol = [_dot3(ts[h], jnp.concatenate([v_ref[:, sls[h]] * bg[:, h:h + 1], kb[h] * egi[h]], axis=1))
           for h in hs]
    s = [s_scr[h] for h in hs]
    ws_qs = [_dot3(jnp.concatenate([sol[h][:, HEAD:], q[h] * egi[h]], axis=0), s[h]) for h in hs]
    v_new = [sol[h][:, :HEAD] - ws_qs[h][:c] for h in hs]
    o = [ws_qs[h][c:] + _dot3(qk[h], v_new[h]) for h in hs]
    for h in hs:
        o_ref[:, sls[h]] = o[h]
    g_last = [gi[h][c - 1:c, :] for h in hs]
    kd = [k[h] * jnp.exp(g_last[h] - gi[h]) for h in hs]
    s_new = [s[h] * jnp.exp(g_last[h]) + _dot3(kd[h], v_new[h], _dot_tn) for h in hs]
    for h in hs:
        s_scr[h] = s_new[h]

    @pl.when(n == pl.num_programs(0) - 1)
    def _():
        s_out_ref[...] = s_scr[...]


def _gdn_chunked(act, bg, s0, heads):
    t_len = act.shape[0]
    c = GDN_CHUNK
    assert t_len % c == 0
    width = heads * HEAD
    n_chunks = t_len // c
    g_t = bg[:, heads:].reshape(n_chunks, c, heads).transpose(0, 2, 1)
    return pl.pallas_call(
        functools.partial(_gdn_chunk_kernel, heads=heads, c=c),
        name="gdn_chunked",
        out_shape=(jax.ShapeDtypeStruct((t_len, width), F32),
                   jax.ShapeDtypeStruct((heads, HEAD, HEAD), F32)),
        grid=(n_chunks,),
        in_specs=[pl.BlockSpec((c, width), lambda n: (n, 0)),
                  pl.BlockSpec((c, width), lambda n: (n, 1)),
                  pl.BlockSpec((c, width), lambda n: (n, 2)),
                  pl.BlockSpec((c, 2 * heads), lambda n: (n, 0)),
                  pl.BlockSpec((1, heads, c), lambda n: (n, 0, 0)),
                  pl.BlockSpec((heads, HEAD, HEAD), lambda n: (0, 0, 0))],
        out_specs=(pl.BlockSpec((c, width), lambda n: (n, 0)),
                   pl.BlockSpec((heads, HEAD, HEAD), lambda n: (0, 0, 0))),
        scratch_shapes=[pltpu.VMEM((heads, HEAD, HEAD), F32)],
        compiler_params=_params("arbitrary"),
    )(act, act, act, bg, g_t, s0)


def _gdn_step_kernel(beta_ref, g_ref, qt_ref, kt_ref, v_ref, s0_ref, o_ref, s_ref, *, seqs, steps, heads):
    bo = pl.program_id(0)
    h = pl.program_id(1)
    for bl in range(seqs):
        s = s0_ref[bl]
        for t in range(steps):
            colx = bl * steps + t
            idx = ((bo * seqs + bl) * steps + t) * heads + h
            beta = beta_ref[idx]
            a = jnp.exp(jnp.full((1, HEAD), g_ref[idx], F32))
            kc = jnp.broadcast_to(kt_ref[:, colx:colx + 1], (HEAD, HEAD))
            qc = jnp.broadcast_to(qt_ref[:, colx:colx + 1], (HEAD, HEAD))
            r = jnp.sum(kc * s, axis=0, keepdims=True)
            u = beta * (v_ref[colx:colx + 1, :] - a * r)
            s = a * s + kc * u
            o_ref[colx:colx + 1, :] = jnp.sum(qc * s, axis=0, keepdims=True)
        s_ref[bl] = s


def _gdn_steps(act_tm, bg, s0, heads, seqs=8):
    steps, b, _ = act_tm.shape
    assert b % seqs == 0
    nbo = b // seqs
    width = heads * HEAD
    lanes = seqs * steps

    def cols(a):
        a = a.reshape(steps, nbo, seqs, heads, HEAD)
        return a.transpose(1, 3, 4, 2, 0).reshape(nbo, heads, HEAD, lanes)

    qt = cols(act_tm[..., :width])
    kt = cols(act_tm[..., width:2 * width])
    v = act_tm[..., 2 * width:].reshape(steps, nbo, seqs, heads, HEAD)
    v = v.transpose(1, 3, 2, 0, 4).reshape(nbo, heads, lanes, HEAD)
    beta = bg[:, :heads].reshape(-1)
    g = bg[:, heads:].reshape(-1)
    smem = pl.BlockSpec(memory_space=pltpu.SMEM)
    o, s_new = pl.pallas_call(
        functools.partial(_gdn_step_kernel, seqs=seqs, steps=steps, heads=heads),
        name="gdn_steps",
        out_shape=(jax.ShapeDtypeStruct((nbo, heads, lanes, HEAD), F32),
                   jax.ShapeDtypeStruct(s0.shape, F32)),
        grid=(nbo, heads),
        in_specs=[smem, smem,
                  pl.BlockSpec((None, None, HEAD, lanes), lambda i, h: (i, h, 0, 0)),
                  pl.BlockSpec((None, None, HEAD, lanes), lambda i, h: (i, h, 0, 0)),
                  pl.BlockSpec((None, None, lanes, HEAD), lambda i, h: (i, h, 0, 0)),
                  pl.BlockSpec((seqs, None, HEAD, HEAD), lambda i, h: (i, h, 0, 0))],
        out_specs=(pl.BlockSpec((None, None, lanes, HEAD), lambda i, h: (i, h, 0, 0)),
                   pl.BlockSpec((seqs, None, HEAD, HEAD), lambda i, h: (i, h, 0, 0))),
        compiler_params=_params("parallel", "parallel"),
    )(beta, g, qt, kt, v, s0)
    o = o.reshape(nbo, heads, seqs, steps, HEAD).transpose(0, 2, 3, 1, 4).reshape(b * steps, width)
    return o, s_new


def _gdn_out_gate(o, z, g):
    outs = []
    for h in range(o.shape[1] // HEAD):
        sl = slice(h * HEAD, (h + 1) * HEAD)
        outs.append(_rms(o[:, sl], g) * _silu(z[:, sl]))
    return jnp.concatenate(outs, axis=1)


def _mm_post_kernel(*refs, gated):
    if gated:
        a_ref, z_ref, ng_ref, w_ref, x_ref, gt_ref, pg_ref, o_ref = refs
        a = _gdn_out_gate(a_ref[...], z_ref[...], ng_ref[...]).astype(BF16)
    else:
        a_ref, w_ref, x_ref, gt_ref, pg_ref, o_ref = refs
        a = a_ref[...].astype(BF16)
    y = _dot(a, w_ref[...])
    o_ref[...] = x_ref[...] + gt_ref[...] * _rms(y, pg_ref[...])


def _matmul_post(a, w, x, gt, post_g, z=None, z_col=0, norm_g=None):
    m, k = a.shape
    d = w.shape[1]
    tm = _tile(m, 256)
    gated = z is not None
    in_specs = [pl.BlockSpec((tm, k), lambda i: (i, 0))]
    args = [a]
    if gated:
        in_specs += [pl.BlockSpec((tm, k), lambda i: (i, z_col)), pl.BlockSpec((1, HEAD), lambda i: (0, 0))]
        args += [z, norm_g.reshape(1, HEAD)]
    in_specs += [pl.BlockSpec((k, d), lambda i: (0, 0)),
                 pl.BlockSpec((tm, d), lambda i: (i, 0)),
                 _mod_spec(gt, tm, d),
                 pl.BlockSpec((1, d), lambda i: (0, 0))]
    args += [w, x, gt, post_g.reshape(1, d)]
    return pl.pallas_call(
        functools.partial(_mm_post_kernel, gated=gated),
        name="matmul_post",
        out_shape=jax.ShapeDtypeStruct((m, d), F32),
        grid=(m // tm,),
        in_specs=in_specs,
        out_specs=pl.BlockSpec((tm, d), lambda i: (i, 0)),
        compiler_params=_params("parallel"),
    )(*args)


def _ffn_kernel(x_ref, g_ref, sc_ref, sh_ref, wg_ref, wu_ref, wd_ref, gt_ref, pg_ref, o_ref, h_scr, acc):
    f = pl.program_id(1)

    @pl.when(f == 0)
    def _():
        h_scr[...] = _norm_mod(x_ref[...], g_ref[...], sc_ref[...], sh_ref[...]).astype(BF16)
        acc[...] = jnp.zeros_like(acc)

    h = h_scr[...]
    act = (_silu(_dot(h, wg_ref[...])) * _dot(h, wu_ref[...])).astype(BF16)
    acc[...] += _dot(act, wd_ref[...])

    @pl.when(f == pl.num_programs(1) - 1)
    def _():
        o_ref[...] = x_ref[...] + gt_ref[...] * _rms(acc[...], pg_ref[...])


def _dense_ffn(x, g, sc, sh, w_gu, w_down, gt, post_g, tm_pref=512, tf_pref=512):
    m, d = x.shape
    ff = w_down.shape[0]
    tm = _tile(m, tm_pref)
    tf = _tile(ff, tf_pref, LANES)
    nf = ff // tf
    return pl.pallas_call(
        _ffn_kernel,
        name="dense_ffn",
        out_shape=jax.ShapeDtypeStruct((m, d), F32),
        grid=(m // tm, nf),
        in_specs=[pl.BlockSpec((tm, d), lambda i, f: (i, 0)),
                  pl.BlockSpec((1, d), lambda i, f: (0, 0)),
                  _mod_spec(sc, tm, d), _mod_spec(sh, tm, d),
                  pl.BlockSpec((d, tf), lambda i, f: (0, f)),
                  pl.BlockSpec((d, tf), lambda i, f: (0, nf + f)),
                  pl.BlockSpec((tf, d), lambda i, f: (f, 0)),
                  _mod_spec(gt, tm, d),
                  pl.BlockSpec((1, d), lambda i, f: (0, 0))],
        out_specs=pl.BlockSpec((tm, d), lambda i, f: (i, 0)),
        scratch_shapes=[pltpu.VMEM((tm, d), BF16), pltpu.VMEM((tm, d), F32)],
        compiler_params=_params("parallel", "arbitrary"),
    )(x, g.reshape(1, d), sc, sh, w_gu, w_gu, w_down, gt, post_g.reshape(1, d))


def _router_kernel(x_ref, g_ref, sc_ref, sh_ref, wr_ref, h_ref, r_ref, *, experts):
    h = _norm_mod(x_ref[...], g_ref[...], sc_ref[...], sh_ref[...])
    h_ref[...] = h.astype(BF16)
    logits = _dot3(h, wr_ref[...])
    lane = lax.broadcasted_iota(jnp.int32, logits.shape, 1)
    lanef = lane.astype(F32)
    logits = jnp.where(lane < experts, logits, NEG_INF)
    m1 = jnp.max(logits, axis=-1, keepdims=True)
    i1 = jnp.min(jnp.where(logits == m1, lanef, float(LANES)), axis=-1, keepdims=True)
    rest = jnp.where(lanef == i1, NEG_INF, logits)
    m2 = jnp.max(rest, axis=-1, keepdims=True)
    i2 = jnp.min(jnp.where(rest == m2, lanef, float(LANES)), axis=-1, keepdims=True)
    e2 = jnp.exp(m2 - m1)
    w1 = 1.0 / (1.0 + e2)
    w2 = e2 / (1.0 + e2)
    r_ref[...] = jnp.where(lane == 0, i1, jnp.where(lane == 1, i2, jnp.where(lane == 2, w1, jnp.where(lane == 3, w2, 0.0))))


def _router(x, g, sc, sh, w_router):
    m, d = x.shape
    experts = w_router.shape[1]
    tm = _tile(m, 512)
    wr = jnp.pad(w_router, ((0, 0), (0, LANES - experts)))
    return pl.pallas_call(
        functools.partial(_router_kernel, experts=experts),
        name="moe_router",
        out_shape=(jax.ShapeDtypeStruct((m, d), BF16), jax.ShapeDtypeStruct((m, LANES), F32)),
        grid=(m // tm,),
        in_specs=[pl.BlockSpec((tm, d), lambda i: (i, 0)),
                  pl.BlockSpec((1, d), lambda i: (0, 0)),
                  _mod_spec(sc, tm, d), _mod_spec(sh, tm, d),
                  pl.BlockSpec((d, LANES), lambda i: (0, 0))],
        out_specs=(pl.BlockSpec((tm, d), lambda i: (i, 0)), pl.BlockSpec((tm, LANES), lambda i: (i, 0))),
        compiler_params=_params("parallel"),
    )(x, g.reshape(1, d), sc, sh, wr)


def _moe_ffn_kernel(te_ref, nu_ref, h_ref, wg_ref, wu_ref, wd_ref, o_ref, acc):
    i = pl.program_id(0)
    f = pl.program_id(1)
    last = pl.num_programs(1) - 1
    used = i < nu_ref[0]

    @pl.when(jnp.logical_and(used, f == 0))
    def _():
        acc[...] = jnp.zeros_like(acc)

    @pl.when(used)
    def _():
        h = h_ref[...]
        act = (_silu(_dot(h, wg_ref[...])) * _dot(h, wu_ref[...])).astype(BF16)
        acc[...] += _dot(act, wd_ref[...])

    @pl.when(jnp.logical_and(used, f == last))
    def _():
        o_ref[...] = acc[...]

    @pl.when(jnp.logical_and(jnp.logical_not(used), f == last))
    def _():
        o_ref[...] = jnp.zeros_like(o_ref)


def _moe_ffn(hs, tile_expert, n_used, w_gu, w_down, tm, tf_pref=512):
    mp, d = hs.shape
    ff = w_down.shape[1]
    tf = _tile(ff, tf_pref, LANES)
    nf = ff // tf

    def fidx(i, f, nu):
        return jnp.where(i < nu[0], f, nf - 1)

    grid_spec = pltpu.PrefetchScalarGridSpec(
        num_scalar_prefetch=2,
        grid=(mp // tm, nf),
        in_specs=[pl.BlockSpec((tm, d), lambda i, f, te, nu: (i, 0)),
                  pl.BlockSpec((None, d, tf), lambda i, f, te, nu: (te[i], 0, fidx(i, f, nu))),
                  pl.BlockSpec((None, d, tf), lambda i, f, te, nu: (te[i], 0, nf + fidx(i, f, nu))),
                  pl.BlockSpec((None, tf, d), lambda i, f, te, nu: (te[i], fidx(i, f, nu), 0))],
        out_specs=pl.BlockSpec((tm, d), lambda i, f, te, nu: (i, 0)),
        scratch_shapes=[pltpu.VMEM((tm, d), F32)],
    )
    return pl.pallas_call(
        _moe_ffn_kernel,
        name="moe_ffn",
        out_shape=jax.ShapeDtypeStruct((mp, d), F32),
        grid_spec=grid_spec,
        compiler_params=_params("arbitrary", "arbitrary"),
    )(tile_expert, n_used, hs, w_gu, w_gu, w_down)


def _combine_kernel(y1_ref, y2_ref, r_ref, x_ref, gt_ref, pg_ref, o_ref):
    r = r_ref[...]
    y = r[:, 2:3] * y1_ref[...] + r[:, 3:4] * y2_ref[...]
    o_ref[...] = x_ref[...] + gt_ref[...] * _rms(y, pg_ref[...])


def _moe_combine(y1, y2, route, x, gt, post_g):
    m, d = x.shape
    tm = _tile(m, 256)
    return pl.pallas_call(
        _combine_kernel,
        name="moe_combine",
        out_shape=jax.ShapeDtypeStruct((m, d), F32),
        grid=(m // tm,),
        in_specs=[pl.BlockSpec((tm, d), lambda i: (i, 0)),
                  pl.BlockSpec((tm, d), lambda i: (i, 0)),
                  pl.BlockSpec((tm, LANES), lambda i: (i, 0)),
                  pl.BlockSpec((tm, d), lambda i: (i, 0)),
                  _mod_spec(gt, tm, d),
                  pl.BlockSpec((1, d), lambda i: (0, 0))],
        out_specs=pl.BlockSpec((tm, d), lambda i: (i, 0)),
        compiler_params=_params("parallel"),
    )(y1, y2, route, x, gt, post_g.reshape(1, d))


def _moe_layer(x, g, sc, sh, gt, post_g, w_router, w_gu, w_down, tm=512):
    m, d = x.shape
    experts = w_router.shape[1]
    h, route = _router(x, g, sc, sh, w_router)
    ids = route[:, :MOE_TOPK].astype(jnp.int32).reshape(-1)
    n_assign = m * MOE_TOPK
    counts = jnp.zeros((experts,), jnp.int32).at[ids].add(1)
    padded = ((counts + tm - 1) // tm) * tm
    starts_p = jnp.cumsum(padded) - padded
    starts = jnp.cumsum(counts) - counts
    order = jnp.argsort(ids, stable=True)
    sorted_ids = ids[order]
    dest_sorted = starts_p[sorted_ids] + (jnp.arange(n_assign, dtype=jnp.int32) - starts[sorted_ids])
    n_tiles = -(-(n_assign + experts * (tm - 1)) // tm)
    mp = n_tiles * tm
    src_tok = jnp.zeros((mp,), jnp.int32).at[dest_sorted].set((order // MOE_TOPK).astype(jnp.int32))
    dest = jnp.zeros((n_assign,), jnp.int32).at[order].set(dest_sorted.astype(jnp.int32))
    tile_start = jnp.arange(n_tiles, dtype=jnp.int32) * tm
    ends_p = jnp.cumsum(padded)
    tile_expert = jnp.minimum(jnp.searchsorted(ends_p, tile_start, side="right"), experts - 1).astype(jnp.int32)
    n_used = (ends_p[-1] // tm).astype(jnp.int32).reshape(1)
    last_e = tile_expert[jnp.maximum(n_used[0] - 1, 0)]
    tile_expert = jnp.where(jnp.arange(n_tiles) < n_used[0], tile_expert, last_e)
    hs = jnp.take(h, src_tok, axis=0)
    ys = _moe_ffn(hs, tile_expert, n_used, w_gu, w_down, tm)
    dest = dest.reshape(m, MOE_TOPK)
    y1 = jnp.take(ys, dest[:, 0], axis=0)
    y2 = jnp.take(ys, dest[:, 1], axis=0)
    return _moe_combine(y1, y2, route, x, gt, post_g)


def _block_mean_kernel(k_ref, o_ref):
    o_ref[0] = jnp.mean(k_ref[...], axis=0, keepdims=True)


def _block_means(k):
    t_len, width = k.shape
    nb = t_len // MOBA_BLOCK
    out = pl.pallas_call(
        _block_mean_kernel,
        name="moba_block_means",
        out_shape=jax.ShapeDtypeStruct((nb, 1, width), F32),
        grid=(nb,),
        in_specs=[pl.BlockSpec((MOBA_BLOCK, width), lambda j: (j, 0))],
        out_specs=pl.BlockSpec((1, 1, width), lambda j: (j, 0, 0)),
        compiler_params=_params("parallel"),
    )(k)
    return out.reshape(nb, width)


def _moba_seq_kernel(q_ref, k_ref, vt_ref, km_ref, o_ref, *, nb, hb):
    bs = MOBA_BLOCK
    ob = pl.program_id(1)
    own = pl.multiple_of(ob * bs, bs)
    causal = lax.broadcasted_iota(jnp.int32, (bs, bs), 0) <= lax.broadcasted_iota(jnp.int32, (bs, bs), 1)
    blk = lax.broadcasted_iota(jnp.int32, (bs, LANES), 1)
    blk_t = lax.broadcasted_iota(jnp.int32, (LANES, bs), 0)
    blkf_t = blk_t.astype(F32)
    hsl = [slice(h * HEAD, (h + 1) * HEAD) for h in range(hb)]

    qts = []
    for h in range(hb):
        q = q_ref[:, hsl[h]]
        gate = _dot3(km_ref[:, hsl[h]], q, _dot_nt)
        gate = jnp.where(blk_t < ob, gate, NEG_INF)
        sel = jnp.zeros((LANES, bs), F32)
        for _ in range(MOBA_TOPK):
            m = jnp.max(gate, axis=0, keepdims=True)
            first = jnp.min(jnp.where(gate == m, blkf_t, float(LANES)), axis=0, keepdims=True)
            pick = jnp.logical_and(blkf_t == first, m > NEG_INF)
            sel = jnp.where(pick, 1.0, sel)
            gate = jnp.where(pick, NEG_INF, gate)
        bias_t = jnp.where(sel > 0.0, 0.0, MASKED)
        q_t = (q * (HEAD ** -0.5)).T
        qts.append(jnp.concatenate([q_t, bias_t], axis=0).astype(BF16))

    def scores(h, start, hot):
        k_ext = jnp.concatenate([k_ref[pl.ds(start, bs), hsl[h]], hot], axis=1)
        return _dot(k_ext, qts[h])

    def one_hot(j):
        return jnp.where(blk == j, 1.0, 0.0).astype(BF16)

    no_hot = jnp.zeros((bs, LANES), BF16)
    ms, ls, accs = [], [], []
    for h in range(hb):
        s = jnp.where(causal, scores(h, own, no_hot), MASKED)
        m = jnp.max(s, axis=0, keepdims=True)
        p = jnp.exp(s - m)
        ms.append(m)
        ls.append(jnp.sum(p, axis=0, keepdims=True))
        accs.append(_dot(vt_ref[ob, hsl[h], :], p.astype(BF16)))

    def body(j, carry):
        ss, ms, ls, accs = carry
        jn = jnp.minimum(j + 1, nb - 1)
        s_next = tuple(scores(h, pl.multiple_of(jn * bs, bs), one_hot(jn)) for h in range(hb))
        m_out, l_out, a_out = [], [], []
        for h in range(hb):
            m_new = jnp.maximum(ms[h], jnp.max(ss[h], axis=0, keepdims=True))
            alpha = jnp.exp(ms[h] - m_new)
            p = jnp.exp(ss[h] - m_new)
            m_out.append(m_new)
            l_out.append(alpha * ls[h] + jnp.sum(p, axis=0, keepdims=True))
            a_out.append(alpha * accs[h] + _dot(vt_ref[j, hsl[h], :], p.astype(BF16)))
        return s_next, tuple(m_out), tuple(l_out), tuple(a_out)

    s_first = tuple(scores(h, 0, one_hot(0)) for h in range(hb))
    _, _, ls, accs = lax.fori_loop(0, ob, body, (s_first, tuple(ms), tuple(ls), tuple(accs)))
    for h in range(hb):
        o_ref[:, hsl[h]] = (accs[h] / ls[h]).T.astype(o_ref.dtype)


def _moba_seq(q, k, v, heads, hb=2):
    t_len, width = q.shape
    bs = MOBA_BLOCK
    assert t_len % bs == 0 and heads % hb == 0
    nb = t_len // bs
    assert nb <= LANES
    kmean = jnp.pad(_block_means(k), ((0, LANES - nb), (0, 0)))
    kb = k.astype(BF16)
    vt = v.astype(BF16).reshape(nb, bs, width).transpose(0, 2, 1)
    return pl.pallas_call(
        functools.partial(_moba_seq_kernel, nb=nb, hb=hb),
        name="moba_seq",
        out_shape=jax.ShapeDtypeStruct((t_len, width), BF16),
        grid=(heads // hb, nb),
        in_specs=[pl.BlockSpec((bs, hb * HEAD), lambda h, i: (i, h)),
                  pl.BlockSpec((t_len, hb * HEAD), lambda h, i: (0, h)),
                  pl.BlockSpec((nb, hb * HEAD, bs), lambda h, i: (0, h, 0)),
                  pl.BlockSpec((LANES, hb * HEAD), lambda h, i: (0, h))],
        out_specs=pl.BlockSpec((bs, hb * HEAD), lambda h, i: (i, h)),
        compiler_params=_params("parallel", "arbitrary"),
    )(q, kb, vt, kmean)


def _head_match_bias(n_rows, heads, cols_used):
    row = lax.broadcasted_iota(jnp.int32, (n_rows, LANES), 0)
    col = lax.broadcasted_iota(jnp.int32, (n_rows, LANES), 1)
    keep = jnp.logical_and(row % heads == col % heads, col < cols_used)
    return keep, row, col


def _moba_paged_kernel(pt_ref, qt_ref, kn_ref, vn_ref, ka_ref, kb_ref, va_ref, vb_ref, o_ref,
                       qtb, bias, acc, ks, st_m, st_l, st_w, *, heads, steps, nblk, page):
    j = pl.program_id(1)
    cols = steps * heads
    n_rows = page * heads

    @pl.when(j == 0)
    def _():
        qtb[...] = (qt_ref[0] * (HEAD ** -0.5)).astype(BF16)
        keep, _, _ = _head_match_bias(n_rows, heads, cols)
        bias[...] = jnp.where(keep, 0.0, MASKED)

    def scores(k_ref):
        return _dot(k_ref[0].astype(BF16), qtb[...]) + bias[...]

    sa = scores(ka_ref)
    sb = scores(kb_ref)
    m = jnp.maximum(jnp.max(sa, axis=0, keepdims=True), jnp.max(sb, axis=0, keepdims=True))
    pa = jnp.exp(sa - m)
    pb = jnp.exp(sb - m)
    l = jnp.sum(pa, axis=0, keepdims=True) + jnp.sum(pb, axis=0, keepdims=True)
    acc[j] = (_dot(pa.T.astype(BF16), va_ref[0].astype(BF16))
              + _dot(pb.T.astype(BF16), vb_ref[0].astype(BF16)))
    ks[j] = (jnp.sum(ka_ref[0].reshape(page, heads, HEAD), axis=0)
             + jnp.sum(kb_ref[0].reshape(page, heads, HEAD), axis=0))
    st_m[pl.ds(j, 1), :] = m
    st_l[pl.ds(j, 1), :] = l

    @pl.when(j == nblk - 1)
    def _():
        qt = qt_ref[0]
        hrow = lax.broadcasted_iota(jnp.int32, (heads, LANES), 0)
        hcol = lax.broadcasted_iota(jnp.int32, (heads, LANES), 1) % heads
        gates = []
        for b in range(nblk):
            g_all = _dot3(ks[b], qt) * (1.0 / (2 * page))
            gates.append(jnp.sum(jnp.where(hrow == hcol, g_all, 0.0), axis=0, keepdims=True))
        sel = []
        for b in range(nblk):
            rank = jnp.zeros((1, LANES), F32)
            for o in range(nblk):
                if o == b:
                    continue
                ahead = gates[o] > gates[b]
                if o < b:
                    ahead = jnp.logical_or(ahead, gates[o] == gates[b])
                rank = rank + jnp.where(ahead, 1.0, 0.0)
            sel.append(rank < float(MOBA_TOPK))
        keep, row, col = _head_match_bias(cols, heads, cols)
        keep = jnp.logical_and(keep, row // heads <= col // heads)
        s_own = _dot(kn_ref[0].astype(BF16), qtb[...]) + jnp.where(keep, 0.0, MASKED)
        m_own = jnp.max(s_own, axis=0, keepdims=True)
        p_own = jnp.exp(s_own - m_own)
        l_own = jnp.sum(p_own, axis=0, keepdims=True)
        v_own = vn_ref[0]
        if cols < LANES:
            p_own = jnp.concatenate([p_own, jnp.zeros((LANES - cols, LANES), F32)], axis=0)
            v_own = jnp.concatenate([v_own, jnp.zeros((LANES - cols, HEAD), F32)], axis=0)
        acc_own = _dot(p_own.T.astype(BF16), v_own.astype(BF16))
        m_all = m_own
        for b in range(nblk):
            m_all = jnp.maximum(m_all, jnp.where(sel[b], st_m[b:b + 1, :], MASKED))
        w_own = jnp.exp(m_own - m_all)
        den = w_own * l_own
        ws = []
        for b in range(nblk):
            wb = jnp.where(sel[b], jnp.exp(st_m[b:b + 1, :] - m_all), 0.0)
            den = den + wb * st_l[b:b + 1, :]
            ws.append(wb)
        inv = 1.0 / den
        st_w[...] = jnp.zeros_like(st_w)
        for b in range(nblk):
            st_w[b:b + 1, :] = ws[b] * inv
        st_w[nblk:nblk + 1, :] = w_own * inv
        eye_r = lax.broadcasted_iota(jnp.int32, (LANES, LANES), 0)
        eye_c = lax.broadcasted_iota(jnp.int32, (LANES, LANES), 1)
        eye = (eye_r == eye_c).astype(BF16)
        wr = st_w[...]
        w_hi, w_mid = _split(wr)
        w_lo = (wr - w_hi.astype(F32) - w_mid.astype(F32)).astype(BF16)
        wcol = _dot_nt(eye, w_hi) + (_dot_nt(eye, w_mid) + _dot_nt(eye, w_lo))
        out = wcol[:, nblk:nblk + 1] * acc_own
        for b in range(nblk):
            out = out + wcol[:, b:b + 1] * acc[b]
        o_ref[0] = out[:cols, :]


def _moba_paged(q, k_new, v_new, cache_k, cache_v, page_table, heads):
    b, steps, width = q.shape
    n_phys, page = cache_k.shape[0], cache_k.shape[1]
    n_pages = page_table.shape[1]
    assert MOBA_BLOCK == 2 * page and n_pages % 2 == 0 and steps <= MOBA_BLOCK
    nblk = n_pages // 2
    cols = steps * heads
    n_rows = page * heads
    stat_rows = -(-(nblk + 1) // 8) * 8
    assert cols <= LANES
    ck = cache_k.reshape(n_phys, n_rows, HEAD)
    cv = cache_v.reshape(n_phys, n_rows, HEAD)
    pt = page_table.reshape(-1).astype(jnp.int32)
    qt = jnp.pad(q.reshape(b, cols, HEAD).transpose(0, 2, 1), ((0, 0), (0, 0), (0, LANES - cols)))
    kn = k_new.reshape(b, cols, HEAD)
    vn = v_new.reshape(b, cols, HEAD)

    def page_spec(off):
        return pl.BlockSpec((1, n_rows, HEAD), lambda i, j, pt: (pt[i * n_pages + 2 * j + off], 0, 0))

    new_spec = pl.BlockSpec((1, cols, HEAD), lambda i, j, pt: (i, 0, 0))
    grid_spec = pltpu.PrefetchScalarGridSpec(
        num_scalar_prefetch=1,
        grid=(b, nblk),
        in_specs=[pl.BlockSpec((1, HEAD, LANES), lambda i, j, pt: (i, 0, 0)), new_spec, new_spec,
                  page_spec(0), page_spec(1), page_spec(0), page_spec(1)],
        out_specs=pl.BlockSpec((1, cols, HEAD), lambda i, j, pt: (i, 0, 0)),
        scratch_shapes=[pltpu.VMEM((HEAD, LANES), BF16), pltpu.VMEM((n_rows, LANES), F32),
                        pltpu.VMEM((nblk, LANES, HEAD), F32), pltpu.VMEM((nblk, heads, HEAD), F32),
                        pltpu.VMEM((stat_rows, LANES), F32), pltpu.VMEM((stat_rows, LANES), F32),
                        pltpu.VMEM((stat_rows, LANES), F32)],
    )
    out = pl.pallas_call(
        functools.partial(_moba_paged_kernel, heads=heads, steps=steps, nblk=nblk, page=page),
        name="moba_paged",
        out_shape=jax.ShapeDtypeStruct((b, cols, HEAD), F32),
        grid_spec=grid_spec,
        compiler_params=_params("parallel", "arbitrary"),
    )(pt, qt, kn, vn, ck, ck, cv, cv)
    return out.reshape(b, steps, width)


def _rope_tables(pos):
    half = HEAD // 2
    inv = ROPE_THETA ** (-jnp.arange(half, dtype=F32) / half)
    ang = pos.astype(F32)[:, None] * inv[None, :]
    cos = jnp.cos(ang)
    sin = jnp.sin(ang)
    return jnp.concatenate([cos, cos], axis=1), jnp.concatenate([-sin, sin], axis=1)


def _chunks(mod, rep):
    parts = jnp.split(mod, 6, axis=-1)
    return [jnp.repeat(p, rep, axis=0) if rep > 1 else p for p in parts]


def kernel(x_prompt, x_sample, cache_k, cache_v, page_table, state_delta, state_conv, c_prompt, c_sample, ada_w, ada_b, ln_mix_pre, ln_mix_post, ln_ffn_pre, ln_ffn_post, a_w_in, a_conv_w, a_A_log, a_dt_bias, a_norm_g, a_w_out, kv_ada_w, kv_ada_b, kv_norm_g, w_kv, b_w_q, b_w_out, f_w_gu, f_w_down, m_router, m_w_gu, m_w_down):
    bp, t_p, d = x_prompt.shape
    bs, t_s, _ = x_sample.shape
    assert bp == 1
    gh = a_A_log.shape[1]
    ah = cache_k.shape[2]
    gw = gh * HEAD
    aw = ah * HEAD
    past = page_table.shape[1] * cache_k.shape[1]

    c_all = jnp.concatenate([c_prompt, c_sample], axis=0)
    r_all = c_all.shape[0]
    c_all = jnp.pad(c_all, ((0, (-r_all) % 8), (0, 0)))
    mods = [_modulation(c_all, ada_w[l], ada_b[l]) for l in range(2)]
    kv_mod = _modulation(c_all, kv_ada_w, kv_ada_b)

    def mod_p(m):
        return m[0:1]

    def mod_s(m):
        return m[1:1 + bs]

    xp = x_prompt.reshape(t_p, d)
    xs = x_sample.reshape(bs * t_s, d)

    w_in = a_w_in[0]
    n_conv = 3 * gw
    w_main = w_in[:, :n_conv + gw].astype(BF16)
    w_ba = w_in[:, n_conv + gw:]
    w_out0 = a_w_out[0].astype(BF16)

    sh_m, sc_m, gt_m, sh_f, sc_f, gt_f = _chunks(mod_p(mods[0]), 1)
    proj_p = _norm_mod_matmul(xp, ln_mix_pre[0], sc_m, sh_m, w_main)
    bg_p = _gdn_gates(xp, ln_mix_pre[0], sc_m, sh_m, w_ba, a_A_log[0], a_dt_bias[0])
    act_p = _gdn_conv_seq(proj_p, a_conv_w[0], jnp.zeros((CONV_W - 1, n_conv), F32), gw)
    o_p, delta_p = _gdn_chunked(act_p, bg_p, jnp.zeros((gh, HEAD, HEAD), F32), gh)
    conv_p = proj_p[t_p - (CONV_W - 1):, :n_conv]
    xp = _matmul_post(o_p, w_out0, xp, gt_m, ln_mix_post[0], z=proj_p, z_col=3, norm_g=a_norm_g[0])

    ssh_m, ssc_m, sgt_m, ssh_f, ssc_f, sgt_f = _chunks(mod_s(mods[0]), t_s)
    proj_s = _norm_mod_matmul(xs, ln_mix_pre[0], ssc_m, ssh_m, w_main)
    bg_s = _gdn_gates(xs, ln_mix_pre[0], ssc_m, ssh_m, w_ba, a_A_log[0], a_dt_bias[0])
    qkv_s = proj_s[:, :n_conv].reshape(bs, t_s, n_conv)
    xc_s = jnp.concatenate([state_conv[:, 0].transpose(1, 0, 2), qkv_s.transpose(1, 0, 2)], axis=0)
    act_s = _gdn_conv_steps(xc_s, a_conv_w[0], gw)
    o_s, delta_s = _gdn_steps(act_s, bg_s, state_delta[:, 0], gh)
    conv_s = qkv_s[:, t_s - (CONV_W - 1):]
    xs = _matmul_post(o_s, w_out0, xs, sgt_m, ln_mix_post[0], z=proj_s, z_col=3, norm_g=a_norm_g[0])

    w_gu0 = f_w_gu[0].astype(BF16)
    w_dn0 = f_w_down[0].astype(BF16)
    xp = _dense_ffn(xp, ln_ffn_pre[0], sc_f, sh_f, w_gu0, w_dn0, gt_f, ln_ffn_post[0])
    xs = _dense_ffn(xs, ln_ffn_pre[0], ssc_f, ssh_f, w_gu0, w_dn0, sgt_f, ln_ffn_post[0])

    w_kv_b = w_kv.astype(BF16)
    cos_p, sin_p = _rope_tables(jnp.arange(t_p))
    cos_s, sin_s = _rope_tables(jnp.tile(past + jnp.arange(t_s), bs))
    ksh, ksc = jnp.split(kv_mod, 2, axis=-1)
    kv_p = _norm_mod_matmul(xp, kv_norm_g, mod_p(ksc), mod_p(ksh), w_kv_b, cos_p, sin_p, rope_cols=aw)
    kv_s = _norm_mod_matmul(xs, kv_norm_g, jnp.repeat(mod_s(ksc), t_s, axis=0), jnp.repeat(mod_s(ksh), t_s, axis=0),
                            w_kv_b, cos_s, sin_s, rope_cols=aw)
    k_p, v_p = kv_p[:, :aw], kv_p[:, aw:]
    k_s, v_s = kv_s[:, :aw], kv_s[:, aw:]

    w_q = b_w_q[0].astype(BF16)
    w_out1 = b_w_out[0].astype(BF16)
    sh_m, sc_m, gt_m, sh_f, sc_f, gt_f = _chunks(mod_p(mods[1]), 1)
    ssh_m, ssc_m, sgt_m, ssh_f, ssc_f, sgt_f = _chunks(mod_s(mods[1]), t_s)
    q_p = _norm_mod_matmul(xp, ln_mix_pre[1], sc_m, sh_m, w_q, cos_p, sin_p, rope_cols=aw)
    q_s = _norm_mod_matmul(xs, ln_mix_pre[1], ssc_m, ssh_m, w_q, cos_s, sin_s, rope_cols=aw)
    att_p = _moba_seq(q_p, k_p, v_p, ah)
    att_s = _moba_paged(q_s.reshape(bs, t_s, aw), k_s.reshape(bs, t_s, aw), v_s.reshape(bs, t_s, aw),
                        cache_k, cache_v, page_table, ah)
    xp = _matmul_post(att_p, w_out1, xp, gt_m, ln_mix_post[1])
    xs = _matmul_post(att_s.reshape(bs * t_s, aw), w_out1, xs, sgt_m, ln_mix_post[1])

    x_all = jnp.concatenate([xp, xs], axis=0)

    def rows(p_mod, s_mod):
        return jnp.concatenate([jnp.broadcast_to(p_mod, (t_p, d)), s_mod], axis=0)

    y_all = _moe_layer(x_all, ln_ffn_pre[1], rows(sc_f, ssc_f), rows(sh_f, ssh_f), rows(gt_f, sgt_f),
                       ln_ffn_post[1], m_router[0], m_w_gu[0].astype(BF16), m_w_down[0].astype(BF16))
    y_p, y_s = y_all[:t_p], y_all[t_p:]

    return (y_p.reshape(1, t_p, d), y_s.reshape(bs, t_s, d),
            delta_p.reshape(1, 1, gh, HEAD, HEAD), conv_p.reshape(1, 1, CONV_W - 1, n_conv),
            k_p.reshape(1, t_p, ah, HEAD), v_p.reshape(1, t_p, ah, HEAD),
            delta_s.reshape(bs, 1, gh, HEAD, HEAD), conv_s.reshape(bs, 1, CONV_W - 1, n_conv),
            k_s.reshape(bs, t_s, ah, HEAD), v_s.reshape(bs, t_s, ah, HEAD))
```

```python
import functools
import math

import jax
import jax.numpy as jnp
from jax import lax
from jax.experimental import pallas as pl
from jax.experimental.pallas import tpu as pltpu

F32 = jnp.float32
BF16 = jnp.bfloat16
NEG_INF = float("-inf")
MASKED = -1e30

EPS = 1e-6
HEAD = 128
GDN_CHUNK = 64
CONV_W = 4
MOBA_BLOCK = 256
MOBA_TOPK = 3
ROPE_THETA = 10000.0
MOE_TOPK = 2
LANES = 128
VMEM_LIMIT = 52 * 1024 * 1024


def _params(*sem):
    return pltpu.CompilerParams(dimension_semantics=sem, vmem_limit_bytes=VMEM_LIMIT)


def _tile(n, pref, mult=8):
    if n <= pref:
        return n
    t = (pref // mult) * mult
    while t >= mult:
        if n % t == 0:
            return t
        t -= mult
    return n


def _dot(a, b):
    return jnp.dot(a, b, preferred_element_type=F32)


def _dot_nt(a, b):
    return lax.dot_general(a, b, (((1,), (1,)), ((), ())), preferred_element_type=F32)


def _dot_tn(a, b):
    return lax.dot_general(a, b, (((0,), (0,)), ((), ())), preferred_element_type=F32)


def _split(a):
    hi = a.astype(BF16)
    lo = (a - hi.astype(F32)).astype(BF16)
    return hi, lo


def _dot3(a, b, dot=_dot):
    ah, al = _split(a)
    bh, bl = _split(b)
    return dot(ah, bh) + (dot(ah, bl) + dot(al, bh))


def _dot_ones(ones, a, left):
    hi, mid = _split(a)
    lo = (a - hi.astype(F32) - mid.astype(F32)).astype(BF16)
    if left:
        return _dot(ones, hi) + (_dot(ones, mid) + _dot(ones, lo))
    return _dot(hi, ones) + (_dot(mid, ones) + _dot(lo, ones))


def _sigmoid(x):
    return 1.0 / (1.0 + jnp.exp(-x))


def _silu(x):
    return x * _sigmoid(x)


def _softplus(x):
    return jnp.maximum(x, 0.0) + jnp.log(1.0 + jnp.exp(-jnp.abs(x)))


def _rms(x, g):
    return x * lax.rsqrt(jnp.mean(x * x, axis=-1, keepdims=True) + EPS) * g


def _norm_mod(x, g, sc, sh):
    return _rms(x, g) * (1.0 + sc) + sh


def _mod_spec(mod, tm, k):
    if mod.shape[0] == 1:
        return pl.BlockSpec((1, k), lambda i, *_: (0, 0))
    return pl.BlockSpec((tm, k), lambda i, *_: (i, 0))


def _mod_kernel(c_ref, w_ref, b_ref, o_ref):
    o_ref[...] = _dot3(_silu(c_ref[...]), w_ref[...]) + b_ref[...]


def _modulation(c, w, b):
    r, d = c.shape
    n = w.shape[1]
    tn = _tile(n, 512, LANES)
    return pl.pallas_call(
        _mod_kernel,
        name="adaln_modulation",
        out_shape=jax.ShapeDtypeStruct((r, n), F32),
        grid=(n // tn,),
        in_specs=[pl.BlockSpec((r, d), lambda j: (0, 0)),
                  pl.BlockSpec((d, tn), lambda j: (0, j)),
                  pl.BlockSpec((1, tn), lambda j: (0, j))],
        out_specs=pl.BlockSpec((r, tn), lambda j: (0, j)),
        compiler_params=_params("parallel"),
    )(c, w, b.reshape(1, n))


def _rope_tile(y, cos, sin):
    outs = []
    for h in range(y.shape[1] // HEAD):
        yh = y[:, h * HEAD:(h + 1) * HEAD]
        outs.append(yh * cos + pltpu.roll(yh, HEAD // 2, 1) * sin)
    return outs[0] if len(outs) == 1 else jnp.concatenate(outs, axis=1)


def _nm_mm_kernel(x_ref, g_ref, sc_ref, sh_ref, w_ref, cos_ref, sin_ref, o_ref, h_scr, *, n_rope):
    j = pl.program_id(1)

    @pl.when(j == 0)
    def _():
        h_scr[...] = _norm_mod(x_ref[...], g_ref[...], sc_ref[...], sh_ref[...]).astype(BF16)

    y = _dot(h_scr[...], w_ref[...].astype(BF16))
    if n_rope == 0:
        o_ref[...] = y
    else:
        @pl.when(j < n_rope)
        def _():
            o_ref[...] = _rope_tile(y, cos_ref[...], sin_ref[...])

        @pl.when(j >= n_rope)
        def _():
            o_ref[...] = y


def _norm_mod_matmul(x, g, sc, sh, w, col0=0, n=None, cos=None, sin=None, rope=False, tm_pref=1024, tn_pref=512):
    m, k = x.shape
    n = w.shape[1] - col0 if n is None else n
    tm = _tile(m, tm_pref)
    tn = _tile(math.gcd(n, col0) if col0 else n, tn_pref, LANES)
    if cos is None:
        cos = jnp.zeros((m, HEAD), F32)
        sin = cos
    assert col0 % tn == 0 and n % tn == 0
    c0 = col0 // tn
    return pl.pallas_call(
        functools.partial(_nm_mm_kernel, n_rope=n // tn if rope else 0),
        name="norm_mod_matmul",
        out_shape=jax.ShapeDtypeStruct((m, n), F32),
        grid=(m // tm, n // tn),
        in_specs=[pl.BlockSpec((tm, k), lambda i, j: (i, 0)),
                  pl.BlockSpec((1, k), lambda i, j: (0, 0)),
                  _mod_spec(sc, tm, k), _mod_spec(sh, tm, k),
                  pl.BlockSpec((k, tn), lambda i, j: (0, c0 + j)),
                  pl.BlockSpec((tm, HEAD), lambda i, j: (i, 0)),
                  pl.BlockSpec((tm, HEAD), lambda i, j: (i, 0))],
        out_specs=pl.BlockSpec((tm, tn), lambda i, j: (i, j)),
        scratch_shapes=[pltpu.VMEM((tm, k), BF16)],
        compiler_params=_params("parallel", "arbitrary"),
    )(x, g.reshape(1, k), sc, sh, w, cos, sin)


def _gates_kernel(x_ref, g_ref, sc_ref, sh_ref, w_ref, alog_ref, dtb_ref, o_ref, *, heads):
    h = _norm_mod(x_ref[...], g_ref[...], sc_ref[...], sh_ref[...])
    ba = _dot3(h, w_ref[...])
    lane = lax.broadcasted_iota(jnp.int32, ba.shape, 1)
    decay = -jnp.exp(alog_ref[...]) * _softplus(ba + dtb_ref[...])
    o_ref[...] = jnp.where(lane < heads, _sigmoid(ba), decay)


def _gdn_gates(x, g, sc, sh, w_ba, a_log, dt_bias):
    m, k = x.shape
    heads = a_log.shape[0]
    tm = _tile(m, 512)
    pad = jnp.zeros((heads,), F32)
    alog2 = jnp.concatenate([pad, a_log]).reshape(1, 2 * heads)
    dtb2 = jnp.concatenate([pad, dt_bias]).reshape(1, 2 * heads)
    return pl.pallas_call(
        functools.partial(_gates_kernel, heads=heads),
        name="gdn_gates",
        out_shape=jax.ShapeDtypeStruct((m, 2 * heads), F32),
        grid=(m // tm,),
        in_specs=[pl.BlockSpec((tm, k), lambda i: (i, 0)),
                  pl.BlockSpec((1, k), lambda i: (0, 0)),
                  _mod_spec(sc, tm, k), _mod_spec(sh, tm, k),
                  pl.BlockSpec((k, 2 * heads), lambda i: (0, 0)),
                  pl.BlockSpec((1, 2 * heads), lambda i: (0, 0)),
                  pl.BlockSpec((1, 2 * heads), lambda i: (0, 0))],
        out_specs=pl.BlockSpec((tm, 2 * heads), lambda i: (i, 0)),
        compiler_params=_params("parallel"),
    )(x, g.reshape(1, k), sc, sh, w_ba, alog2, dtb2)


def _qkv_act(acc, part):
    act = _silu(acc)
    outs = []
    for h in range(act.shape[1] // HEAD):
        a = act[:, h * HEAD:(h + 1) * HEAD]
        inv = lax.rsqrt(jnp.sum(a * a, axis=-1, keepdims=True) + EPS)
        fac = jnp.where(part == 0, inv * (HEAD ** -0.5), jnp.where(part == 1, inv, 1.0))
        outs.append(a * fac)
    return jnp.concatenate(outs, axis=1)


def _conv_seq_kernel(x_ref, w_ref, b0_ref, o_ref, xs, *, tt):
    part = pl.program_id(0)
    t = pl.program_id(1)

    @pl.when(t == 0)
    def _():
        xs[0:8, :] = b0_ref[...]

    @pl.when(t > 0)
    def _():
        xs[0:8, :] = xs[tt:tt + 8, :]

    xs[8:8 + tt, :] = x_ref[...]
    w = w_ref[...]
    acc = xs[5:5 + tt, :] * w[0:1, :]
    for j in range(1, CONV_W):
        acc = acc + xs[5 + j:5 + j + tt, :] * w[j:j + 1, :]
    o_ref[...] = _qkv_act(acc, part)


def _gdn_conv_seq(qkvz, conv_w, buf, width):
    t_len = qkvz.shape[0]
    tt = _tile(t_len, 512)
    b0 = jnp.concatenate([jnp.zeros((8 - (CONV_W - 1), 3 * width), F32), buf], axis=0)
    return pl.pallas_call(
        functools.partial(_conv_seq_kernel, tt=tt),
        name="gdn_conv_seq",
        out_shape=jax.ShapeDtypeStruct((t_len, 3 * width), F32),
        grid=(3, t_len // tt),
        in_specs=[pl.BlockSpec((tt, width), lambda c, t: (t, c)),
                  pl.BlockSpec((CONV_W, width), lambda c, t: (0, c)),
                  pl.BlockSpec((8, width), lambda c, t: (0, c))],
        out_specs=pl.BlockSpec((tt, width), lambda c, t: (t, c)),
        scratch_shapes=[pltpu.VMEM((tt + 8, width), F32)],
        compiler_params=_params("parallel", "arbitrary"),
    )(qkvz, conv_w, b0)


def _conv_step_kernel(x_ref, w_ref, o_ref, *, steps):
    part = pl.program_id(0)
    w = w_ref[...]
    for t in range(steps):
        acc = x_ref[t] * w[0:1, :]
        for j in range(1, CONV_W):
            acc = acc + x_ref[t + j] * w[j:j + 1, :]
        o_ref[t] = _qkv_act(acc, part)


def _gdn_conv_steps(xc, conv_w, width):
    rows, b, _ = xc.shape
    steps = rows - (CONV_W - 1)
    return pl.pallas_call(
        functools.partial(_conv_step_kernel, steps=steps),
        name="gdn_conv_steps",
        out_shape=jax.ShapeDtypeStruct((steps, b, 3 * width), F32),
        grid=(3,),
        in_specs=[pl.BlockSpec((rows, b, width), lambda c: (0, 0, c)),
                  pl.BlockSpec((CONV_W, width), lambda c: (0, c))],
        out_specs=pl.BlockSpec((steps, b, width), lambda c: (0, 0, c)),
        compiler_params=_params("parallel"),
    )(xc, conv_w)


def _unit_lower_inverses(mats, c):
    row = lax.broadcasted_iota(jnp.int32, (c, c), 0)
    col = lax.broadcasted_iota(jnp.int32, (c, c), 1)
    eye = (row == col).astype(F32)
    base = min(c, 16)
    same = (row // base) == (col // base)
    xs = [jnp.where(same, a, 0.0) for a in mats]
    ts = [eye - x for x in xs]
    for _ in range(int(math.log2(base)) - 1):
        xs = [_dot3(x, x) for x in xs]
        ts = [t + _dot3(t, x) for t, x in zip(ts, xs)]
    size = base
    while size < c:
        inner = same
        size *= 2
        same = (row // size) == (col // size)
        pick = jnp.logical_and(same, jnp.logical_not(inner))
        ys = [_dot3(jnp.where(pick, a, 0.0), t) for a, t in zip(mats, ts)]
        ts = [t - _dot3(t, y) for t, y in zip(ts, ys)]
    return ts


def _gdn_chunk_kernel(q_ref, k_ref, v_ref, bg_ref, gt_ref, s0_ref, o_ref, s_out_ref, s_scr, *, heads, c):
    n = pl.program_id(0)

    @pl.when(n == 0)
    def _():
        s_scr[...] = s0_ref[...]

    row = lax.broadcasted_iota(jnp.int32, (c, c), 0)
    col = lax.broadcasted_iota(jnp.int32, (c, c), 1)
    lower = row >= col
    strict = row > col
    bg = bg_ref[...]
    gcum = _dot_ones(lower.astype(BF16), bg, left=True)
    gcum_t = _dot_ones((row <= col).astype(BF16), gt_ref[0], left=False)

    hs = range(heads)
    sls = [slice(h * HEAD, (h + 1) * HEAD) for h in hs]
    q = [q_ref[:, sl] for sl in sls]
    k = [k_ref[:, sl] for sl in sls]
    gi = [gcum[:, heads + h:heads + h + 1] for h in hs]
    decay = [jnp.exp(jnp.where(lower, gi[h] - gcum_t[h:h + 1, :], NEG_INF)) for h in hs]
    kb = [k[h] * bg[:, h:h + 1] for h in hs]
    egi = [jnp.exp(g) for g in gi]
    qk_kk = [_dot3(jnp.concatenate([q[h], kb[h]], axis=0), k[h], _dot_nt) for h in hs]
    qk = [qk_kk[h][:c] * decay[h] for h in hs]
    ts = _unit_lower_inverses([jnp.where(strict, qk_kk[h][c:] * decay[h], 0.0) for h in hs], c)
    sol = [_dot3(ts[h], jnp.concatenate([v_ref[:, sls[h]] * bg[:, h:h + 1], kb[h] * egi[h]], axis=1))
           for h in hs]
    s = [s_scr[h] for h in hs]
    ws_qs = [_dot3(jnp.concatenate([sol[h][:, HEAD:], q[h] * egi[h]], axis=0), s[h]) for h in hs]
    v_new = [sol[h][:, :HEAD] - ws_qs[h][:c] for h in hs]
    o = [ws_qs[h][c:] + _dot3(qk[h], v_new[h]) for h in hs]
    for h in hs:
        o_ref[:, sls[h]] = o[h]
    g_last = [gi[h][c - 1:c, :] for h in hs]
    kd = [k[h] * jnp.exp(g_last[h] - gi[h]) for h in hs]
    s_new = [s[h] * jnp.exp(g_last[h]) + _dot3(kd[h], v_new[h], _dot_tn) for h in hs]
    for h in hs:
        s_scr[h] = s_new[h]

    @pl.when(n == pl.num_programs(0) - 1)
    def _():
        s_out_ref[...] = s_scr[...]


def _gdn_chunked(act, bg, s0, heads):
    t_len = act.shape[0]
    c = GDN_CHUNK
    assert t_len % c == 0
    width = heads * HEAD
    n_chunks = t_len // c
    g_t = bg[:, heads:].reshape(n_chunks, c, heads).transpose(0, 2, 1)
    return pl.pallas_call(
        functools.partial(_gdn_chunk_kernel, heads=heads, c=c),
        name="gdn_chunked",
        out_shape=(jax.ShapeDtypeStruct((t_len, width), F32),
                   jax.ShapeDtypeStruct((heads, HEAD, HEAD), F32)),
        grid=(n_chunks,),
        in_specs=[pl.BlockSpec((c, width), lambda n: (n, 0)),
                  pl.BlockSpec((c, width), lambda n: (n, 1)),
                  pl.BlockSpec((c, width), lambda n: (n, 2)),
                  pl.BlockSpec((c, 2 * heads), lambda n: (n, 0)),
                  pl.BlockSpec((1, heads, c), lambda n: (n, 0, 0)),
                  pl.BlockSpec((heads, HEAD, HEAD), lambda n: (0, 0, 0))],
        out_specs=(pl.BlockSpec((c, width), lambda n: (n, 0)),
                   pl.BlockSpec((heads, HEAD, HEAD), lambda n: (0, 0, 0))),
        scratch_shapes=[pltpu.VMEM((heads, HEAD, HEAD), F32)],
        compiler_params=_params("arbitrary"),
    )(act, act, act, bg, g_t, s0)


def _gdn_step_kernel(beta_ref, g_ref, qt_ref, kt_ref, v_ref, s0_ref, o_ref, s_ref, *, seqs, steps, heads):
    bo = pl.program_id(0)
    h = pl.program_id(1)
    for bl in range(seqs):
        s = s0_ref[bl]
        for t in range(steps):
            colx = bl * steps + t
            idx = ((bo * seqs + bl) * steps + t) * heads + h
            beta = beta_ref[idx]
            a = jnp.exp(jnp.full((1, HEAD), g_ref[idx], F32))
            kc = jnp.broadcast_to(kt_ref[:, colx:colx + 1], (HEAD, HEAD))
            qc = jnp.broadcast_to(qt_ref[:, colx:colx + 1], (HEAD, HEAD))
            r = jnp.sum(kc * s, axis=0, keepdims=True)
            u = beta * (v_ref[colx:colx + 1, :] - a * r)
            s = a * s + kc * u
            o_ref[colx:colx + 1, :] = jnp.sum(qc * s, axis=0, keepdims=True)
        s_ref[bl] = s


def _gdn_steps(act_tm, bg, s0, heads, seqs=8):
    steps, b, _ = act_tm.shape
    assert b % seqs == 0
    nbo = b // seqs
    width = heads * HEAD
    lanes = seqs * steps

    def cols(a):
        a = a.reshape(steps, nbo, seqs, heads, HEAD)
        return a.transpose(1, 3, 4, 2, 0).reshape(nbo, heads, HEAD, lanes)

    qt = cols(act_tm[..., :width])
    kt = cols(act_tm[..., width:2 * width])
    v = act_tm[..., 2 * width:].reshape(steps, nbo, seqs, heads, HEAD)
    v = v.transpose(1, 3, 2, 0, 4).reshape(nbo, heads, lanes, HEAD)
    beta = bg[:, :heads].reshape(-1)
    g = bg[:, heads:].reshape(-1)
    smem = pl.BlockSpec(memory_space=pltpu.SMEM)
    o, s_new = pl.pallas_call(
        functools.partial(_gdn_step_kernel, seqs=seqs, steps=steps, heads=heads),
        name="gdn_steps",
        out_shape=(jax.ShapeDtypeStruct((nbo, heads, lanes, HEAD), F32),
                   jax.ShapeDtypeStruct(s0.shape, F32)),
        grid=(nbo, heads),
        in_specs=[smem, smem,
                  pl.BlockSpec((None, None, HEAD, lanes), lambda i, h: (i, h, 0, 0)),
                  pl.BlockSpec((None, None, HEAD, lanes), lambda i, h: (i, h, 0, 0)),
                  pl.BlockSpec((None, None, lanes, HEAD), lambda i, h: (i, h, 0, 0)),
                  pl.BlockSpec((seqs, None, HEAD, HEAD), lambda i, h: (i, h, 0, 0))],
        out_specs=(pl.BlockSpec((None, None, lanes, HEAD), lambda i, h: (i, h, 0, 0)),
                   pl.BlockSpec((seqs, None, HEAD, HEAD), lambda i, h: (i, h, 0, 0))),
        compiler_params=_params("parallel", "parallel"),
    )(beta, g, qt, kt, v, s0)
    o = o.reshape(nbo, heads, seqs, steps, HEAD).transpose(0, 2, 3, 1, 4).reshape(b * steps, width)
    return o, s_new


def _gdn_out_gate(o, z, g):
    outs = []
    for h in range(o.shape[1] // HEAD):
        sl = slice(h * HEAD, (h + 1) * HEAD)
        outs.append(_rms(o[:, sl], g) * _silu(z[:, sl]))
    return jnp.concatenate(outs, axis=1)


def _mm_post_kernel(*refs, gated):
    if gated:
        a_ref, z_ref, ng_ref, w_ref, x_ref, gt_ref, pg_ref, o_ref = refs
        a = _gdn_out_gate(a_ref[...], z_ref[...], ng_ref[...]).astype(BF16)
    else:
        a_ref, w_ref, x_ref, gt_ref, pg_ref, o_ref = refs
        a = a_ref[...].astype(BF16)
    y = _dot(a, w_ref[...])
    o_ref[...] = x_ref[...] + gt_ref[...] * _rms(y, pg_ref[...])


def _matmul_post(a, w, x, gt, post_g, z=None, z_col=0, norm_g=None):
    m, k = a.shape
    d = w.shape[1]
    tm = _tile(m, 256)
    gated = z is not None
    in_specs = [pl.BlockSpec((tm, k), lambda i: (i, 0))]
    args = [a]
    if gated:
        in_specs += [pl.BlockSpec((tm, k), lambda i: (i, z_col)), pl.BlockSpec((1, HEAD), lambda i: (0, 0))]
        args += [z, norm_g.reshape(1, HEAD)]
    in_specs += [pl.BlockSpec((k, d), lambda i: (0, 0)),
                 pl.BlockSpec((tm, d), lambda i: (i, 0)),
                 _mod_spec(gt, tm, d),
                 pl.BlockSpec((1, d), lambda i: (0, 0))]
    args += [w, x, gt, post_g.reshape(1, d)]
    return pl.pallas_call(
        functools.partial(_mm_post_kernel, gated=gated),
        name="matmul_post",
        out_shape=jax.ShapeDtypeStruct((m, d), F32),
        grid=(m // tm,),
        in_specs=in_specs,
        out_specs=pl.BlockSpec((tm, d), lambda i: (i, 0)),
        compiler_params=_params("parallel"),
    )(*args)


def _ffn_kernel(x_ref, g_ref, sc_ref, sh_ref, wg_ref, wu_ref, wd_ref, gt_ref, pg_ref, o_ref, h_scr, acc):
    f = pl.program_id(1)

    @pl.when(f == 0)
    def _():
        h_scr[...] = _norm_mod(x_ref[...], g_ref[...], sc_ref[...], sh_ref[...]).astype(BF16)
        acc[...] = jnp.zeros_like(acc)

    h = h_scr[...]
    act = (_silu(_dot(h, wg_ref[...])) * _dot(h, wu_ref[...])).astype(BF16)
    acc[...] += _dot(act, wd_ref[...])

    @pl.when(f == pl.num_programs(1) - 1)
    def _():
        o_ref[...] = x_ref[...] + gt_ref[...] * _rms(acc[...], pg_ref[...])


def _dense_ffn(x, g, sc, sh, w_gu, w_down, gt, post_g, tm_pref=512, tf_pref=512):
    m, d = x.shape
    ff = w_down.shape[0]
    tm = _tile(m, tm_pref)
    tf = _tile(ff, tf_pref, LANES)
    nf = ff // tf
    return pl.pallas_call(
        _ffn_kernel,
        name="dense_ffn",
        out_shape=jax.ShapeDtypeStruct((m, d), F32),
        grid=(m // tm, nf),
        in_specs=[pl.BlockSpec((tm, d), lambda i, f: (i, 0)),
                  pl.BlockSpec((1, d), lambda i, f: (0, 0)),
                  _mod_spec(sc, tm, d), _mod_spec(sh, tm, d),
                  pl.BlockSpec((d, tf), lambda i, f: (0, f)),
                  pl.BlockSpec((d, tf), lambda i, f: (0, nf + f)),
                  pl.BlockSpec((tf, d), lambda i, f: (f, 0)),
                  _mod_spec(gt, tm, d),
                  pl.BlockSpec((1, d), lambda i, f: (0, 0))],
        out_specs=pl.BlockSpec((tm, d), lambda i, f: (i, 0)),
        scratch_shapes=[pltpu.VMEM((tm, d), BF16), pltpu.VMEM((tm, d), F32)],
        compiler_params=_params("parallel", "arbitrary"),
    )(x, g.reshape(1, d), sc, sh, w_gu, w_gu, w_down, gt, post_g.reshape(1, d))


def _router_kernel(x_ref, g_ref, sc_ref, sh_ref, wr_ref, h_ref, r_ref, *, experts):
    h = _norm_mod(x_ref[...], g_ref[...], sc_ref[...], sh_ref[...])
    h_ref[...] = h.astype(BF16)
    logits = _dot3(h, wr_ref[...])
    lane = lax.broadcasted_iota(jnp.int32, logits.shape, 1)
    lanef = lane.astype(F32)
    logits = jnp.where(lane < experts, logits, NEG_INF)
    m1 = jnp.max(logits, axis=-1, keepdims=True)
    i1 = jnp.min(jnp.where(logits == m1, lanef, float(LANES)), axis=-1, keepdims=True)
    rest = jnp.where(lanef == i1, NEG_INF, logits)
    m2 = jnp.max(rest, axis=-1, keepdims=True)
    i2 = jnp.min(jnp.where(rest == m2, lanef, float(LANES)), axis=-1, keepdims=True)
    e2 = jnp.exp(m2 - m1)
    w1 = 1.0 / (1.0 + e2)
    w2 = e2 / (1.0 + e2)
    r_ref[...] = jnp.where(lane == 0, i1, jnp.where(lane == 1, i2, jnp.where(lane == 2, w1, jnp.where(lane == 3, w2, 0.0))))


def _router(x, g, sc, sh, w_router):
    m, d = x.shape
    experts = w_router.shape[1]
    tm = _tile(m, 512)
    wr = jnp.pad(w_router, ((0, 0), (0, LANES - experts)))
    return pl.pallas_call(
        functools.partial(_router_kernel, experts=experts),
        name="moe_router",
        out_shape=(jax.ShapeDtypeStruct((m, d), BF16), jax.ShapeDtypeStruct((m, LANES), F32)),
        grid=(m // tm,),
        in_specs=[pl.BlockSpec((tm, d), lambda i: (i, 0)),
                  pl.BlockSpec((1, d), lambda i: (0, 0)),
                  _mod_spec(sc, tm, d), _mod_spec(sh, tm, d),
                  pl.BlockSpec((d, LANES), lambda i: (0, 0))],
        out_specs=(pl.BlockSpec((tm, d), lambda i: (i, 0)), pl.BlockSpec((tm, LANES), lambda i: (i, 0))),
        compiler_params=_params("parallel"),
    )(x, g.reshape(1, d), sc, sh, wr)


def _moe_ffn_kernel(te_ref, nu_ref, h_ref, wg_ref, wu_ref, wd_ref, o_ref, acc):
    i = pl.program_id(0)
    f = pl.program_id(1)
    last = pl.num_programs(1) - 1
    used = i < nu_ref[0]

    @pl.when(jnp.logical_and(used, f == 0))
    def _():
        acc[...] = jnp.zeros_like(acc)

    @pl.when(used)
    def _():
        h = h_ref[...]
        act = (_silu(_dot(h, wg_ref[...].astype(BF16))) * _dot(h, wu_ref[...].astype(BF16))).astype(BF16)
        acc[...] += _dot(act, wd_ref[...].astype(BF16))

    @pl.when(jnp.logical_and(used, f == last))
    def _():
        o_ref[...] = acc[...]

    @pl.when(jnp.logical_and(jnp.logical_not(used), f == last))
    def _():
        o_ref[...] = jnp.zeros_like(o_ref)


def _moe_ffn(hs, tile_expert, n_used, w_gu, w_down, tm, tf_pref=256):
    mp, d = hs.shape
    ff = w_down.shape[1]
    tf = _tile(ff, tf_pref, LANES)
    nf = ff // tf

    def fidx(i, f, nu):
        return jnp.where(i < nu[0], f, nf - 1)

    grid_spec = pltpu.PrefetchScalarGridSpec(
        num_scalar_prefetch=2,
        grid=(mp // tm, nf),
        in_specs=[pl.BlockSpec((tm, d), lambda i, f, te, nu: (i, 0)),
                  pl.BlockSpec((None, d, tf), lambda i, f, te, nu: (te[i], 0, fidx(i, f, nu))),
                  pl.BlockSpec((None, d, tf), lambda i, f, te, nu: (te[i], 0, nf + fidx(i, f, nu))),
                  pl.BlockSpec((None, tf, d), lambda i, f, te, nu: (te[i], fidx(i, f, nu), 0))],
        out_specs=pl.BlockSpec((tm, d), lambda i, f, te, nu: (i, 0)),
        scratch_shapes=[pltpu.VMEM((tm, d), F32)],
    )
    return pl.pallas_call(
        _moe_ffn_kernel,
        name="moe_ffn",
        out_shape=jax.ShapeDtypeStruct((mp, d), F32),
        grid_spec=grid_spec,
        compiler_params=_params("arbitrary", "arbitrary"),
    )(tile_expert, n_used, hs, w_gu, w_gu, w_down)


def _combine_kernel(y1_ref, y2_ref, r_ref, x_ref, gt_ref, pg_ref, o_ref):
    r = r_ref[...]
    y = r[:, 2:3] * y1_ref[...] + r[:, 3:4] * y2_ref[...]
    o_ref[...] = x_ref[...] + gt_ref[...] * _rms(y, pg_ref[...])


def _moe_combine(y1, y2, route, x, gt, post_g):
    m, d = x.shape
    tm = _tile(m, 256)
    return pl.pallas_call(
        _combine_kernel,
        name="moe_combine",
        out_shape=jax.ShapeDtypeStruct((m, d), F32),
        grid=(m // tm,),
        in_specs=[pl.BlockSpec((tm, d), lambda i: (i, 0)),
                  pl.BlockSpec((tm, d), lambda i: (i, 0)),
                  pl.BlockSpec((tm, LANES), lambda i: (i, 0)),
                  pl.BlockSpec((tm, d), lambda i: (i, 0)),
                  _mod_spec(gt, tm, d),
                  pl.BlockSpec((1, d), lambda i: (0, 0))],
        out_specs=pl.BlockSpec((tm, d), lambda i: (i, 0)),
        compiler_params=_params("parallel"),
    )(y1, y2, route, x, gt, post_g.reshape(1, d))


def _moe_layer(parts, g, post_g, w_router, w_gu, w_down, tm=768):
    experts = w_router.shape[1]
    routed = [_router(x, g, sc, sh, w_router) for x, sc, sh, _ in parts]
    h = jnp.concatenate([r[0] for r in routed], axis=0)
    route = jnp.concatenate([r[1] for r in routed], axis=0)
    m = h.shape[0]
    ids = route[:, :MOE_TOPK].astype(jnp.int32).reshape(-1)
    n_assign = m * MOE_TOPK
    counts = jnp.zeros((experts,), jnp.int32).at[ids].add(1)
    padded = ((counts + tm - 1) // tm) * tm
    starts_p = jnp.cumsum(padded) - padded
    starts = jnp.cumsum(counts) - counts
    order = jnp.argsort(ids, stable=True)
    sorted_ids = ids[order]
    dest_sorted = starts_p[sorted_ids] + (jnp.arange(n_assign, dtype=jnp.int32) - starts[sorted_ids])
    n_tiles = -(-(n_assign + experts * (tm - 1)) // tm)
    mp = n_tiles * tm
    src_tok = jnp.zeros((mp,), jnp.int32).at[dest_sorted].set((order // MOE_TOPK).astype(jnp.int32))
    dest = jnp.zeros((n_assign,), jnp.int32).at[order].set(dest_sorted.astype(jnp.int32))
    tile_start = jnp.arange(n_tiles, dtype=jnp.int32) * tm
    ends_p = jnp.cumsum(padded)
    tile_expert = jnp.minimum(jnp.searchsorted(ends_p, tile_start, side="right"), experts - 1).astype(jnp.int32)
    n_used = (ends_p[-1] // tm).astype(jnp.int32).reshape(1)
    last_e = tile_expert[jnp.maximum(n_used[0] - 1, 0)]
    tile_expert = jnp.where(jnp.arange(n_tiles) < n_used[0], tile_expert, last_e)
    hs = jnp.take(h, src_tok, axis=0)
    ys = _moe_ffn(hs, tile_expert, n_used, w_gu, w_down, tm)
    dest = dest.reshape(m, MOE_TOPK)
    outs = []
    row0 = 0
    for (x, _, _, gt), (_, route_x) in zip(parts, routed):
        rows = slice(row0, row0 + x.shape[0])
        row0 += x.shape[0]
        y1 = jnp.take(ys, dest[rows, 0], axis=0)
        y2 = jnp.take(ys, dest[rows, 1], axis=0)
        outs.append(_moe_combine(y1, y2, route_x, x, gt, post_g))
    return outs


def _block_mean_kernel(k_ref, o_ref):
    o_ref[0] = jnp.mean(k_ref[...], axis=0, keepdims=True)


def _block_means(k):
    t_len, width = k.shape
    nb = t_len // MOBA_BLOCK
    out = pl.pallas_call(
        _block_mean_kernel,
        name="moba_block_means",
        out_shape=jax.ShapeDtypeStruct((nb, 1, width), F32),
        grid=(nb,),
        in_specs=[pl.BlockSpec((MOBA_BLOCK, width), lambda j: (j, 0))],
        out_specs=pl.BlockSpec((1, 1, width), lambda j: (j, 0, 0)),
        compiler_params=_params("parallel"),
    )(k)
    return out.reshape(nb, width)


def _moba_seq_kernel(q_ref, k_ref, vt_ref, km_ref, o_ref, *, nb, hb):
    bs = MOBA_BLOCK
    ob = pl.program_id(1)
    own = pl.multiple_of(ob * bs, bs)
    causal = lax.broadcasted_iota(jnp.int32, (bs, bs), 0) <= lax.broadcasted_iota(jnp.int32, (bs, bs), 1)
    blk = lax.broadcasted_iota(jnp.int32, (bs, LANES), 1)
    blk_t = lax.broadcasted_iota(jnp.int32, (LANES, bs), 0)
    blkf_t = blk_t.astype(F32)
    hsl = [slice(h * HEAD, (h + 1) * HEAD) for h in range(hb)]

    qts = []
    for h in range(hb):
        q = q_ref[:, hsl[h]]
        gate = _dot3(km_ref[:, hsl[h]], q, _dot_nt)
        gate = jnp.where(blk_t < ob, gate, NEG_INF)
        sel = jnp.zeros((LANES, bs), F32)
        for _ in range(MOBA_TOPK):
            m = jnp.max(gate, axis=0, keepdims=True)
            first = jnp.min(jnp.where(gate == m, blkf_t, float(LANES)), axis=0, keepdims=True)
            pick = jnp.logical_and(blkf_t == first, m > NEG_INF)
            sel = jnp.where(pick, 1.0, sel)
            gate = jnp.where(pick, NEG_INF, gate)
        bias_t = jnp.where(sel > 0.0, 0.0, MASKED)
        q_t = (q * (HEAD ** -0.5 * math.log2(math.e))).T
        qts.append(jnp.concatenate([q_t, bias_t], axis=0).astype(BF16))

    def scores(h, start, hot):
        k_ext = jnp.concatenate([k_ref[pl.ds(start, bs), hsl[h]], hot], axis=1)
        return _dot(k_ext, qts[h])

    def one_hot(j):
        return jnp.where(blk == j, 1.0, 0.0).astype(BF16)

    no_hot = jnp.zeros((bs, LANES), BF16)
    ms, ls, accs = [], [], []
    for h in range(hb):
        s = jnp.where(causal, scores(h, own, no_hot), MASKED)
        m = jnp.max(s, axis=0, keepdims=True)
        p = jnp.exp2(s - m)
        ms.append(m)
        ls.append(jnp.sum(p, axis=0, keepdims=True))
        accs.append(_dot(vt_ref[ob, hsl[h], :], p.astype(BF16)))

    def body(j, carry):
        ss, ms, ls, accs = carry
        jn = jnp.minimum(j + 1, nb - 1)
        s_next = tuple(scores(h, pl.multiple_of(jn * bs, bs), one_hot(jn)) for h in range(hb))
        m_out, l_out, a_out = [], [], []
        for h in range(hb):
            m_new = jnp.maximum(ms[h], jnp.max(ss[h], axis=0, keepdims=True))
            alpha = jnp.exp2(ms[h] - m_new)
            p = jnp.exp2(ss[h] - m_new)
            m_out.append(m_new)
            l_out.append(alpha * ls[h] + jnp.sum(p, axis=0, keepdims=True))
            a_out.append(alpha * accs[h] + _dot(vt_ref[j, hsl[h], :], p.astype(BF16)))
        return s_next, tuple(m_out), tuple(l_out), tuple(a_out)

    s_first = tuple(scores(h, 0, one_hot(0)) for h in range(hb))
    _, _, ls, accs = lax.fori_loop(0, ob, body, (s_first, tuple(ms), tuple(ls), tuple(accs)))
    for h in range(hb):
        o_ref[:, hsl[h]] = (accs[h] / ls[h]).T.astype(o_ref.dtype)


def _moba_seq(q, k, v, heads, hb=4):
    t_len, width = q.shape
    bs = MOBA_BLOCK
    assert t_len % bs == 0 and heads % hb == 0
    nb = t_len // bs
    assert nb <= LANES
    kmean = jnp.pad(_block_means(k), ((0, LANES - nb), (0, 0)))
    kb = k.astype(BF16)
    vt = v.astype(BF16).reshape(nb, bs, width).transpose(0, 2, 1)
    return pl.pallas_call(
        functools.partial(_moba_seq_kernel, nb=nb, hb=hb),
        name="moba_seq",
        out_shape=jax.ShapeDtypeStruct((t_len, width), BF16),
        grid=(heads // hb, nb),
        in_specs=[pl.BlockSpec((bs, hb * HEAD), lambda h, i: (i, h)),
                  pl.BlockSpec((t_len, hb * HEAD), lambda h, i: (0, h)),
                  pl.BlockSpec((nb, hb * HEAD, bs), lambda h, i: (0, h, 0)),
                  pl.BlockSpec((LANES, hb * HEAD), lambda h, i: (0, h))],
        out_specs=pl.BlockSpec((bs, hb * HEAD), lambda h, i: (i, h)),
        compiler_params=_params("parallel", "arbitrary"),
    )(q, kb, vt, kmean)


def _head_match_bias(n_rows, heads, cols_used):
    row = lax.broadcasted_iota(jnp.int32, (n_rows, LANES), 0)
    col = lax.broadcasted_iota(jnp.int32, (n_rows, LANES), 1)
    keep = jnp.logical_and(row % heads == col % heads, col < cols_used)
    return keep, row, col


def _moba_paged_kernel(pt_ref, qt_ref, kn_ref, vn_ref, ka_ref, kb_ref, va_ref, vb_ref, o_ref,
                       qtb, bias, acc, ks, st_m, st_l, st_w, *, heads, steps, nblk, page):
    j = pl.program_id(1)
    cols = steps * heads
    n_rows = page * heads

    @pl.when(j == 0)
    def _():
        qtb[...] = (qt_ref[0] * (HEAD ** -0.5 * math.log2(math.e))).astype(BF16)
        keep, _, _ = _head_match_bias(n_rows, heads, cols)
        bias[...] = jnp.where(keep, 0.0, MASKED)

    def scores(k_ref):
        return _dot(k_ref[0].astype(BF16), qtb[...]) + bias[...]

    sa = scores(ka_ref)
    sb = scores(kb_ref)
    m = jnp.maximum(jnp.max(sa, axis=0, keepdims=True), jnp.max(sb, axis=0, keepdims=True))
    pa = jnp.exp2(sa - m)
    pb = jnp.exp2(sb - m)
    l = jnp.sum(pa, axis=0, keepdims=True) + jnp.sum(pb, axis=0, keepdims=True)
    acc[j] = (_dot(pa.T.astype(BF16), va_ref[0].astype(BF16))
              + _dot(pb.T.astype(BF16), vb_ref[0].astype(BF16)))
    ks[j] = (jnp.sum(ka_ref[0].reshape(page, heads, HEAD), axis=0)
             + jnp.sum(kb_ref[0].reshape(page, heads, HEAD), axis=0))
    st_m[pl.ds(j, 1), :] = m
    st_l[pl.ds(j, 1), :] = l

    @pl.when(j == nblk - 1)
    def _():
        qt = qt_ref[0]
        hrow = lax.broadcasted_iota(jnp.int32, (heads, LANES), 0)
        hcol = lax.broadcasted_iota(jnp.int32, (heads, LANES), 1) % heads
        gates = []
        for b in range(nblk):
            g_all = _dot3(ks[b], qt) * (1.0 / (2 * page))
            gates.append(jnp.sum(jnp.where(hrow == hcol, g_all, 0.0), axis=0, keepdims=True))
        sel = []
        for b in range(nblk):
            rank = jnp.zeros((1, LANES), F32)
            for o in range(nblk):
                if o == b:
                    continue
                ahead = gates[o] > gates[b]
                if o < b:
                    ahead = jnp.logical_or(ahead, gates[o] == gates[b])
                rank = rank + jnp.where(ahead, 1.0, 0.0)
            sel.append(rank < float(MOBA_TOPK))
        keep, row, col = _head_match_bias(cols, heads, cols)
        keep = jnp.logical_and(keep, row // heads <= col // heads)
        s_own = _dot(kn_ref[0].astype(BF16), qtb[...]) + jnp.where(keep, 0.0, MASKED)
        m_own = jnp.max(s_own, axis=0, keepdims=True)
        p_own = jnp.exp2(s_own - m_own)
        l_own = jnp.sum(p_own, axis=0, keepdims=True)
        v_own = vn_ref[0]
        if cols < LANES:
            p_own = jnp.concatenate([p_own, jnp.zeros((LANES - cols, LANES), F32)], axis=0)
            v_own = jnp.concatenate([v_own, jnp.zeros((LANES - cols, HEAD), F32)], axis=0)
        acc_own = _dot(p_own.T.astype(BF16), v_own.astype(BF16))
        m_all = m_own
        for b in range(nblk):
            m_all = jnp.maximum(m_all, jnp.where(sel[b], st_m[b:b + 1, :], MASKED))
        w_own = jnp.exp2(m_own - m_all)
        den = w_own * l_own
        ws = []
        for b in range(nblk):
            wb = jnp.where(sel[b], jnp.exp2(st_m[b:b + 1, :] - m_all), 0.0)
            den = den + wb * st_l[b:b + 1, :]
            ws.append(wb)
        inv = 1.0 / den
        st_w[...] = jnp.zeros_like(st_w)
        for b in range(nblk):
            st_w[b:b + 1, :] = ws[b] * inv
        st_w[nblk:nblk + 1, :] = w_own * inv
        eye_r = lax.broadcasted_iota(jnp.int32, (LANES, LANES), 0)
        eye_c = lax.broadcasted_iota(jnp.int32, (LANES, LANES), 1)
        eye = (eye_r == eye_c).astype(BF16)
        wr = st_w[...]
        w_hi, w_mid = _split(wr)
        w_lo = (wr - w_hi.astype(F32) - w_mid.astype(F32)).astype(BF16)
        wcol = _dot_nt(eye, w_hi) + (_dot_nt(eye, w_mid) + _dot_nt(eye, w_lo))
        out = wcol[:, nblk:nblk + 1] * acc_own
        for b in range(nblk):
            out = out + wcol[:, b:b + 1] * acc[b]
        o_ref[0] = out[:cols, :]


def _moba_paged(q, k_new, v_new, cache_k, cache_v, page_table, heads):
    b, steps, width = q.shape
    n_phys, page = cache_k.shape[0], cache_k.shape[1]
    n_pages = page_table.shape[1]
    assert MOBA_BLOCK == 2 * page and n_pages % 2 == 0 and steps <= MOBA_BLOCK
    nblk = n_pages // 2
    cols = steps * heads
    n_rows = page * heads
    stat_rows = -(-(nblk + 1) // 8) * 8
    assert cols <= LANES
    ck = cache_k.reshape(n_phys, n_rows, HEAD)
    cv = cache_v.reshape(n_phys, n_rows, HEAD)
    pt = page_table.reshape(-1).astype(jnp.int32)
    qt = jnp.pad(q.reshape(b, cols, HEAD).transpose(0, 2, 1), ((0, 0), (0, 0), (0, LANES - cols)))
    kn = k_new.reshape(b, cols, HEAD)
    vn = v_new.reshape(b, cols, HEAD)

    def page_spec(off):
        return pl.BlockSpec((1, n_rows, HEAD), lambda i, j, pt: (pt[i * n_pages + 2 * j + off], 0, 0))

    new_spec = pl.BlockSpec((1, cols, HEAD), lambda i, j, pt: (i, 0, 0))
    grid_spec = pltpu.PrefetchScalarGridSpec(
        num_scalar_prefetch=1,
        grid=(b, nblk),
        in_specs=[pl.BlockSpec((1, HEAD, LANES), lambda i, j, pt: (i, 0, 0)), new_spec, new_spec,
                  page_spec(0), page_spec(1), page_spec(0), page_spec(1)],
        out_specs=pl.BlockSpec((1, cols, HEAD), lambda i, j, pt: (i, 0, 0)),
        scratch_shapes=[pltpu.VMEM((HEAD, LANES), BF16), pltpu.VMEM((n_rows, LANES), F32),
                        pltpu.VMEM((nblk, LANES, HEAD), F32), pltpu.VMEM((nblk, heads, HEAD), F32),
                        pltpu.VMEM((stat_rows, LANES), F32), pltpu.VMEM((stat_rows, LANES), F32),
                        pltpu.VMEM((stat_rows, LANES), F32)],
    )
    out = pl.pallas_call(
        functools.partial(_moba_paged_kernel, heads=heads, steps=steps, nblk=nblk, page=page),
        name="moba_paged",
        out_shape=jax.ShapeDtypeStruct((b, cols, HEAD), F32),
        grid_spec=grid_spec,
        compiler_params=_params("parallel", "arbitrary"),
    )(pt, qt, kn, vn, ck, ck, cv, cv)
    return out.reshape(b, steps, width)


def _rope_tables(pos):
    half = HEAD // 2
    inv = ROPE_THETA ** (-jnp.arange(half, dtype=F32) / half)
    ang = pos.astype(F32)[:, None] * inv[None, :]
    cos = jnp.cos(ang)
    sin = jnp.sin(ang)
    return jnp.concatenate([cos, cos], axis=1), jnp.concatenate([-sin, sin], axis=1)


def _chunks(mod, rep):
    parts = jnp.split(mod, 6, axis=-1)
    return [jnp.repeat(p, rep, axis=0) if rep > 1 else p for p in parts]


def kernel(x_prompt, x_sample, cache_k, cache_v, page_table, state_delta, state_conv, c_prompt, c_sample, ada_w, ada_b, ln_mix_pre, ln_mix_post, ln_ffn_pre, ln_ffn_post, a_w_in, a_conv_w, a_A_log, a_dt_bias, a_norm_g, a_w_out, kv_ada_w, kv_ada_b, kv_norm_g, w_kv, b_w_q, b_w_out, f_w_gu, f_w_down, m_router, m_w_gu, m_w_down):
    bp, t_p, d = x_prompt.shape
    bs, t_s, _ = x_sample.shape
    assert bp == 1
    gh = a_A_log.shape[1]
    ah = cache_k.shape[2]
    gw = gh * HEAD
    aw = ah * HEAD
    past = page_table.shape[1] * cache_k.shape[1]

    c_all = jnp.concatenate([c_prompt, c_sample], axis=0)
    r_all = c_all.shape[0]
    c_all = jnp.pad(c_all, ((0, (-r_all) % 8), (0, 0)))
    mods = [_modulation(c_all, ada_w[l], ada_b[l]) for l in range(2)]
    kv_mod = _modulation(c_all, kv_ada_w, kv_ada_b)

    def mod_p(m):
        return m[0:1]

    def mod_s(m):
        return m[1:1 + bs]

    xp = x_prompt.reshape(t_p, d)
    xs = x_sample.reshape(bs * t_s, d)

    w_in = a_w_in[0]
    n_conv = 3 * gw
    w_ba = w_in[:, n_conv + gw:]
    w_out0 = a_w_out[0].astype(BF16)

    sh_m, sc_m, gt_m, sh_f, sc_f, gt_f = _chunks(mod_p(mods[0]), 1)
    proj_p = _norm_mod_matmul(xp, ln_mix_pre[0], sc_m, sh_m, w_in, n=n_conv + gw)
    bg_p = _gdn_gates(xp, ln_mix_pre[0], sc_m, sh_m, w_ba, a_A_log[0], a_dt_bias[0])
    act_p = _gdn_conv_seq(proj_p, a_conv_w[0], jnp.zeros((CONV_W - 1, n_conv), F32), gw)
    o_p, delta_p = _gdn_chunked(act_p, bg_p, jnp.zeros((gh, HEAD, HEAD), F32), gh)
    conv_p = proj_p[t_p - (CONV_W - 1):, :n_conv]
    xp = _matmul_post(o_p, w_out0, xp, gt_m, ln_mix_post[0], z=proj_p, z_col=3, norm_g=a_norm_g[0])

    ssh_m, ssc_m, sgt_m, ssh_f, ssc_f, sgt_f = _chunks(mod_s(mods[0]), t_s)
    proj_s = _norm_mod_matmul(xs, ln_mix_pre[0], ssc_m, ssh_m, w_in, n=n_conv + gw)
    bg_s = _gdn_gates(xs, ln_mix_pre[0], ssc_m, ssh_m, w_ba, a_A_log[0], a_dt_bias[0])
    qkv_s = proj_s[:, :n_conv].reshape(bs, t_s, n_conv)
    xc_s = jnp.concatenate([state_conv[:, 0].transpose(1, 0, 2), qkv_s.transpose(1, 0, 2)], axis=0)
    act_s = _gdn_conv_steps(xc_s, a_conv_w[0], gw)
    o_s, delta_s = _gdn_steps(act_s, bg_s, state_delta[:, 0], gh)
    conv_s = qkv_s[:, t_s - (CONV_W - 1):]
    xs = _matmul_post(o_s, w_out0, xs, sgt_m, ln_mix_post[0], z=proj_s, z_col=3, norm_g=a_norm_g[0])

    w_gu0 = f_w_gu[0].astype(BF16)
    w_dn0 = f_w_down[0].astype(BF16)
    xp = _dense_ffn(xp, ln_ffn_pre[0], sc_f, sh_f, w_gu0, w_dn0, gt_f, ln_ffn_post[0])
    xs = _dense_ffn(xs, ln_ffn_pre[0], ssc_f, ssh_f, w_gu0, w_dn0, sgt_f, ln_ffn_post[0])

    cos_p, sin_p = _rope_tables(jnp.arange(t_p))
    cos_s, sin_s = _rope_tables(jnp.tile(past + jnp.arange(t_s), bs))
    ksh, ksc = jnp.split(kv_mod, 2, axis=-1)
    ksc_s = jnp.repeat(mod_s(ksc), t_s, axis=0)
    ksh_s = jnp.repeat(mod_s(ksh), t_s, axis=0)
    k_p = _norm_mod_matmul(xp, kv_norm_g, mod_p(ksc), mod_p(ksh), w_kv, 0, aw, cos_p, sin_p, rope=True)
    v_p = _norm_mod_matmul(xp, kv_norm_g, mod_p(ksc), mod_p(ksh), w_kv, aw, aw)
    k_s = _norm_mod_matmul(xs, kv_norm_g, ksc_s, ksh_s, w_kv, 0, aw, cos_s, sin_s, rope=True)
    v_s = _norm_mod_matmul(xs, kv_norm_g, ksc_s, ksh_s, w_kv, aw, aw)

    w_out1 = b_w_out[0].astype(BF16)
    sh_m, sc_m, gt_m, sh_f, sc_f, gt_f = _chunks(mod_p(mods[1]), 1)
    ssh_m, ssc_m, sgt_m, ssh_f, ssc_f, sgt_f = _chunks(mod_s(mods[1]), t_s)
    q_p = _norm_mod_matmul(xp, ln_mix_pre[1], sc_m, sh_m, b_w_q[0], cos=cos_p, sin=sin_p, rope=True)
    q_s = _norm_mod_matmul(xs, ln_mix_pre[1], ssc_m, ssh_m, b_w_q[0], cos=cos_s, sin=sin_s, rope=True)
    att_p = _moba_seq(q_p, k_p, v_p, ah)
    att_s = _moba_paged(q_s.reshape(bs, t_s, aw), k_s.reshape(bs, t_s, aw), v_s.reshape(bs, t_s, aw),
                        cache_k, cache_v, page_table, ah)
    xp = _matmul_post(att_p, w_out1, xp, gt_m, ln_mix_post[1])
    xs = _matmul_post(att_s.reshape(bs * t_s, aw), w_out1, xs, sgt_m, ln_mix_post[1])

    y_p, y_s = _moe_layer([(xp, sc_f, sh_f, gt_f), (xs, ssc_f, ssh_f, sgt_f)], ln_ffn_pre[1], ln_ffn_post[1],
                          m_router[0], m_w_gu[0], m_w_down[0])

    return (y_p.reshape(1, t_p, d), y_s.reshape(bs, t_s, d),
            delta_p.reshape(1, 1, gh, HEAD, HEAD), conv_p.reshape(1, 1, CONV_W - 1, n_conv),
            k_p.reshape(1, t_p, ah, HEAD), v_p.reshape(1, t_p, ah, HEAD),
            delta_s.reshape(bs, 1, gh, HEAD, HEAD), conv_s.reshape(bs, 1, CONV_W - 1, n_conv),
            k_s.reshape(bs, t_s, ah, HEAD), v_s.reshape(bs, t_s, ah, HEAD))
```

```python
import functools
import math

import jax
import jax.numpy as jnp
from jax import lax
from jax.experimental import pallas as pl
from jax.experimental.pallas import tpu as pltpu

F32 = jnp.float32
BF16 = jnp.bfloat16
NEG_INF = float("-inf")
MASKED = -1e30

EPS = 1e-6
HEAD = 128
GDN_CHUNK = 64
CONV_W = 4
MOBA_BLOCK = 256
MOBA_TOPK = 3
ROPE_THETA = 10000.0
MOE_TOPK = 2
LANES = 128
VMEM_LIMIT = 52 * 1024 * 1024


def _params(*sem):
    return pltpu.CompilerParams(dimension_semantics=sem, vmem_limit_bytes=VMEM_LIMIT)


def _tile(n, pref, mult=8):
    if n <= pref:
        return n
    t = (pref // mult) * mult
    while t >= mult:
        if n % t == 0:
            return t
        t -= mult
    return n


def _dot(a, b):
    return jnp.dot(a, b, preferred_element_type=F32)


def _dot_nt(a, b):
    return lax.dot_general(a, b, (((1,), (1,)), ((), ())), preferred_element_type=F32)


def _dot_tn(a, b):
    return lax.dot_general(a, b, (((0,), (0,)), ((), ())), preferred_element_type=F32)


def _split(a):
    hi = a.astype(BF16)
    lo = (a - hi.astype(F32)).astype(BF16)
    return hi, lo


def _dot3(a, b, dot=_dot):
    ah, al = _split(a)
    bh, bl = _split(b)
    return dot(ah, bh) + (dot(ah, bl) + dot(al, bh))


def _dot_ones(ones, a, left):
    hi, mid = _split(a)
    lo = (a - hi.astype(F32) - mid.astype(F32)).astype(BF16)
    if left:
        return _dot(ones, hi) + (_dot(ones, mid) + _dot(ones, lo))
    return _dot(hi, ones) + (_dot(mid, ones) + _dot(lo, ones))


def _sigmoid(x):
    return 1.0 / (1.0 + jnp.exp(-x))


def _silu(x):
    return x * _sigmoid(x)


def _softplus(x):
    return jnp.maximum(x, 0.0) + jnp.log(1.0 + jnp.exp(-jnp.abs(x)))


def _rms(x, g):
    return x * lax.rsqrt(jnp.mean(x * x, axis=-1, keepdims=True) + EPS) * g


def _norm_mod(x, g, sc, sh):
    return _rms(x, g) * (1.0 + sc) + sh


def _mod_spec(mod, tm, k):
    if mod.shape[0] == 1:
        return pl.BlockSpec((1, k), lambda i, *_: (0, 0))
    return pl.BlockSpec((tm, k), lambda i, *_: (i, 0))


def _mod_kernel(c_ref, w_ref, b_ref, o_ref):
    o_ref[...] = _dot3(_silu(c_ref[...]), w_ref[...]) + b_ref[...]


def _modulation(c, w, b, layer=0):
    r, d = c.shape
    n = w.shape[2]
    tn = _tile(n, 512, LANES)
    return pl.pallas_call(
        _mod_kernel,
        name="adaln_modulation",
        out_shape=jax.ShapeDtypeStruct((r, n), F32),
        grid=(n // tn,),
        in_specs=[pl.BlockSpec((r, d), lambda j: (0, 0)),
                  pl.BlockSpec((None, d, tn), lambda j: (layer, 0, j)),
                  pl.BlockSpec((None, 1, tn), lambda j: (layer, 0, j))],
        out_specs=pl.BlockSpec((r, tn), lambda j: (0, j)),
        compiler_params=_params("parallel"),
    )(c, w, b.reshape(b.shape[0], 1, n))


def _rope_tile(y, cos, sin):
    outs = []
    for h in range(y.shape[1] // HEAD):
        yh = y[:, h * HEAD:(h + 1) * HEAD]
        outs.append(yh * cos + pltpu.roll(yh, HEAD // 2, 1) * sin)
    return outs[0] if len(outs) == 1 else jnp.concatenate(outs, axis=1)


def _nm_mm_kernel(x_ref, g_ref, sc_ref, sh_ref, w_ref, cos_ref, sin_ref, o_ref, h_scr, *, n_rope):
    j = pl.program_id(1)

    @pl.when(j == 0)
    def _():
        h_scr[...] = _norm_mod(x_ref[...], g_ref[...], sc_ref[...], sh_ref[...]).astype(BF16)

    y = _dot(h_scr[...], w_ref[...].astype(BF16))
    if n_rope == 0:
        o_ref[...] = y
    else:
        @pl.when(j < n_rope)
        def _():
            o_ref[...] = _rope_tile(y, cos_ref[...], sin_ref[...])

        @pl.when(j >= n_rope)
        def _():
            o_ref[...] = y


def _norm_mod_matmul(x, g, sc, sh, w, col0=0, n=None, cos=None, sin=None, rope=False, tm_pref=1024, tn_pref=512):
    m, k = x.shape
    n = w.shape[1] - col0 if n is None else n
    tm = _tile(m, tm_pref)
    tn = _tile(math.gcd(n, col0) if col0 else n, tn_pref, LANES)
    if cos is None:
        cos = jnp.zeros((m, HEAD), F32)
        sin = cos
    assert col0 % tn == 0 and n % tn == 0
    c0 = col0 // tn
    return pl.pallas_call(
        functools.partial(_nm_mm_kernel, n_rope=n // tn if rope else 0),
        name="norm_mod_matmul",
        out_shape=jax.ShapeDtypeStruct((m, n), F32),
        grid=(m // tm, n // tn),
        in_specs=[pl.BlockSpec((tm, k), lambda i, j: (i, 0)),
                  pl.BlockSpec((1, k), lambda i, j: (0, 0)),
                  _mod_spec(sc, tm, k), _mod_spec(sh, tm, k),
                  pl.BlockSpec((k, tn), lambda i, j: (0, c0 + j)),
                  pl.BlockSpec((tm, HEAD), lambda i, j: (i, 0)),
                  pl.BlockSpec((tm, HEAD), lambda i, j: (i, 0))],
        out_specs=pl.BlockSpec((tm, tn), lambda i, j: (i, j)),
        scratch_shapes=[pltpu.VMEM((tm, k), BF16)],
        compiler_params=_params("parallel", "arbitrary"),
    )(x, g.reshape(1, k), sc, sh, w, cos, sin)


def _gates_kernel(x_ref, g_ref, sc_ref, sh_ref, w_ref, alog_ref, dtb_ref, o_ref, *, heads):
    h = _norm_mod(x_ref[...], g_ref[...], sc_ref[...], sh_ref[...])
    ba = _dot3(h, w_ref[...])
    lane = lax.broadcasted_iota(jnp.int32, ba.shape, 1)
    decay = -jnp.exp(alog_ref[...]) * _softplus(ba + dtb_ref[...])
    o_ref[...] = jnp.where(lane < heads, _sigmoid(ba), decay)


def _gdn_gates(x, g, sc, sh, w_ba, a_log, dt_bias):
    m, k = x.shape
    heads = a_log.shape[0]
    tm = _tile(m, 512)
    pad = jnp.zeros((heads,), F32)
    alog2 = jnp.concatenate([pad, a_log]).reshape(1, 2 * heads)
    dtb2 = jnp.concatenate([pad, dt_bias]).reshape(1, 2 * heads)
    return pl.pallas_call(
        functools.partial(_gates_kernel, heads=heads),
        name="gdn_gates",
        out_shape=jax.ShapeDtypeStruct((m, 2 * heads), F32),
        grid=(m // tm,),
        in_specs=[pl.BlockSpec((tm, k), lambda i: (i, 0)),
                  pl.BlockSpec((1, k), lambda i: (0, 0)),
                  _mod_spec(sc, tm, k), _mod_spec(sh, tm, k),
                  pl.BlockSpec((k, 2 * heads), lambda i: (0, 0)),
                  pl.BlockSpec((1, 2 * heads), lambda i: (0, 0)),
                  pl.BlockSpec((1, 2 * heads), lambda i: (0, 0))],
        out_specs=pl.BlockSpec((tm, 2 * heads), lambda i: (i, 0)),
        compiler_params=_params("parallel"),
    )(x, g.reshape(1, k), sc, sh, w_ba, alog2, dtb2)


def _qkv_act(acc, part):
    act = _silu(acc)
    outs = []
    for h in range(act.shape[1] // HEAD):
        a = act[:, h * HEAD:(h + 1) * HEAD]
        inv = lax.rsqrt(jnp.sum(a * a, axis=-1, keepdims=True) + EPS)
        fac = jnp.where(part == 0, inv * (HEAD ** -0.5), jnp.where(part == 1, inv, 1.0))
        outs.append(a * fac)
    return jnp.concatenate(outs, axis=1)


def _conv_seq_kernel(x_ref, w_ref, b0_ref, o_ref, xs, *, tt):
    part = pl.program_id(0)
    t = pl.program_id(1)

    @pl.when(t == 0)
    def _():
        xs[0:8, :] = b0_ref[...]

    @pl.when(t > 0)
    def _():
        xs[0:8, :] = xs[tt:tt + 8, :]

    xs[8:8 + tt, :] = x_ref[...]
    w = w_ref[...]
    acc = xs[5:5 + tt, :] * w[0:1, :]
    for j in range(1, CONV_W):
        acc = acc + xs[5 + j:5 + j + tt, :] * w[j:j + 1, :]
    o_ref[...] = _qkv_act(acc, part)


def _gdn_conv_seq(qkvz, conv_w, buf, width):
    t_len = qkvz.shape[0]
    tt = _tile(t_len, 512)
    b0 = jnp.concatenate([jnp.zeros((8 - (CONV_W - 1), 3 * width), F32), buf], axis=0)
    return pl.pallas_call(
        functools.partial(_conv_seq_kernel, tt=tt),
        name="gdn_conv_seq",
        out_shape=jax.ShapeDtypeStruct((t_len, 3 * width), F32),
        grid=(3, t_len // tt),
        in_specs=[pl.BlockSpec((tt, width), lambda c, t: (t, c)),
                  pl.BlockSpec((CONV_W, width), lambda c, t: (0, c)),
                  pl.BlockSpec((8, width), lambda c, t: (0, c))],
        out_specs=pl.BlockSpec((tt, width), lambda c, t: (t, c)),
        scratch_shapes=[pltpu.VMEM((tt + 8, width), F32)],
        compiler_params=_params("parallel", "arbitrary"),
    )(qkvz, conv_w, b0)


def _conv_step_kernel(x_ref, w_ref, o_ref, *, steps):
    part = pl.program_id(0)
    w = w_ref[...]
    for t in range(steps):
        acc = x_ref[t] * w[0:1, :]
        for j in range(1, CONV_W):
            acc = acc + x_ref[t + j] * w[j:j + 1, :]
        o_ref[t] = _qkv_act(acc, part)


def _gdn_conv_steps(xc, conv_w, width):
    rows, b, _ = xc.shape
    steps = rows - (CONV_W - 1)
    return pl.pallas_call(
        functools.partial(_conv_step_kernel, steps=steps),
        name="gdn_conv_steps",
        out_shape=jax.ShapeDtypeStruct((steps, b, 3 * width), F32),
        grid=(3,),
        in_specs=[pl.BlockSpec((rows, b, width), lambda c: (0, 0, c)),
                  pl.BlockSpec((CONV_W, width), lambda c: (0, c))],
        out_specs=pl.BlockSpec((steps, b, width), lambda c: (0, 0, c)),
        compiler_params=_params("parallel"),
    )(xc, conv_w)


def _unit_lower_inverses(mats, c):
    row = lax.broadcasted_iota(jnp.int32, (c, c), 0)
    col = lax.broadcasted_iota(jnp.int32, (c, c), 1)
    eye = (row == col).astype(F32)
    base = min(c, 16)
    same = (row // base) == (col // base)
    xs = [jnp.where(same, a, 0.0) for a in mats]
    ts = [eye - x for x in xs]
    for _ in range(int(math.log2(base)) - 1):
        xs = [_dot3(x, x) for x in xs]
        ts = [t + _dot3(t, x) for t, x in zip(ts, xs)]
    size = base
    while size < c:
        inner = same
        size *= 2
        same = (row // size) == (col // size)
        pick = jnp.logical_and(same, jnp.logical_not(inner))
        ys = [_dot3(jnp.where(pick, a, 0.0), t) for a, t in zip(mats, ts)]
        ts = [t - _dot3(t, y) for t, y in zip(ts, ys)]
    return ts


def _gdn_chunk_kernel(q_ref, k_ref, v_ref, bg_ref, gt_ref, s0_ref, o_ref, s_out_ref, s_scr, *, heads, c):
    n = pl.program_id(0)

    @pl.when(n == 0)
    def _():
        s_scr[...] = s0_ref[...]

    row = lax.broadcasted_iota(jnp.int32, (c, c), 0)
    col = lax.broadcasted_iota(jnp.int32, (c, c), 1)
    lower = row >= col
    strict = row > col
    bg = bg_ref[...]
    gcum = _dot_ones(lower.astype(BF16), bg, left=True)
    gcum_t = _dot_ones((row <= col).astype(BF16), gt_ref[0], left=False)

    hs = range(heads)
    sls = [slice(h * HEAD, (h + 1) * HEAD) for h in hs]
    q = [q_ref[:, sl] for sl in sls]
    k = [k_ref[:, sl] for sl in sls]
    gi = [gcum[:, heads + h:heads + h + 1] for h in hs]
    decay = [jnp.exp(jnp.where(lower, gi[h] - gcum_t[h:h + 1, :], NEG_INF)) for h in hs]
    kb = [k[h] * bg[:, h:h + 1] for h in hs]
    egi = [jnp.exp(g) for g in gi]
    qk_kk = [_dot3(jnp.concatenate([q[h], kb[h]], axis=0), k[h], _dot_nt) for h in hs]
    qk = [qk_kk[h][:c] * decay[h] for h in hs]
    ts = _unit_lower_inverses([jnp.where(strict, qk_kk[h][c:] * decay[h], 0.0) for h in hs], c)
    sol = [_dot3(ts[h], jnp.concatenate([v_ref[:, sls[h]] * bg[:, h:h + 1], kb[h] * egi[h]], axis=1))
           for h in hs]
    s = [s_scr[h] for h in hs]
    ws_qs = [_dot3(jnp.concatenate([sol[h][:, HEAD:], q[h] * egi[h]], axis=0), s[h]) for h in hs]
    v_new = [sol[h][:, :HEAD] - ws_qs[h][:c] for h in hs]
    o = [ws_qs[h][c:] + _dot3(qk[h], v_new[h]) for h in hs]
    for h in hs:
        o_ref[:, sls[h]] = o[h]
    g_last = [gi[h][c - 1:c, :] for h in hs]
    kd = [k[h] * jnp.exp(g_last[h] - gi[h]) for h in hs]
    s_new = [s[h] * jnp.exp(g_last[h]) + _dot3(kd[h], v_new[h], _dot_tn) for h in hs]
    for h in hs:
        s_scr[h] = s_new[h]

    @pl.when(n == pl.num_programs(0) - 1)
    def _():
        s_out_ref[...] = s_scr[...]


def _gdn_chunked(act, bg, s0, heads):
    t_len = act.shape[0]
    c = GDN_CHUNK
    assert t_len % c == 0
    width = heads * HEAD
    n_chunks = t_len // c
    g_t = bg[:, heads:].reshape(n_chunks, c, heads).transpose(0, 2, 1)
    return pl.pallas_call(
        functools.partial(_gdn_chunk_kernel, heads=heads, c=c),
        name="gdn_chunked",
        out_shape=(jax.ShapeDtypeStruct((t_len, width), F32),
                   jax.ShapeDtypeStruct((heads, HEAD, HEAD), F32)),
        grid=(n_chunks,),
        in_specs=[pl.BlockSpec((c, width), lambda n: (n, 0)),
                  pl.BlockSpec((c, width), lambda n: (n, 1)),
                  pl.BlockSpec((c, width), lambda n: (n, 2)),
                  pl.BlockSpec((c, 2 * heads), lambda n: (n, 0)),
                  pl.BlockSpec((1, heads, c), lambda n: (n, 0, 0)),
                  pl.BlockSpec((heads, HEAD, HEAD), lambda n: (0, 0, 0))],
        out_specs=(pl.BlockSpec((c, width), lambda n: (n, 0)),
                   pl.BlockSpec((heads, HEAD, HEAD), lambda n: (0, 0, 0))),
        scratch_shapes=[pltpu.VMEM((heads, HEAD, HEAD), F32)],
        compiler_params=_params("arbitrary"),
    )(act, act, act, bg, g_t, s0)


def _gdn_step_kernel(beta_ref, g_ref, qt_ref, kt_ref, v_ref, s0_ref, o_ref, s_ref, *, seqs, steps, heads):
    bo = pl.program_id(0)
    h = pl.program_id(1)
    for bl in range(seqs):
        s = s0_ref[bl]
        for t in range(steps):
            colx = bl * steps + t
            idx = ((bo * seqs + bl) * steps + t) * heads + h
            beta = beta_ref[idx]
            a = jnp.exp(jnp.full((1, HEAD), g_ref[idx], F32))
            kc = jnp.broadcast_to(kt_ref[:, colx:colx + 1], (HEAD, HEAD))
            qc = jnp.broadcast_to(qt_ref[:, colx:colx + 1], (HEAD, HEAD))
            r = jnp.sum(kc * s, axis=0, keepdims=True)
            u = beta * (v_ref[colx:colx + 1, :] - a * r)
            s = a * s + kc * u
            o_ref[colx:colx + 1, :] = jnp.sum(qc * s, axis=0, keepdims=True)
        s_ref[bl] = s


def _gdn_steps(act_tm, bg, s0, heads, seqs=8):
    steps, b, _ = act_tm.shape
    assert b % seqs == 0
    nbo = b // seqs
    width = heads * HEAD
    lanes = seqs * steps

    def cols(a):
        a = a.reshape(steps, nbo, seqs, heads, HEAD)
        return a.transpose(1, 3, 4, 2, 0).reshape(nbo, heads, HEAD, lanes)

    qt = cols(act_tm[..., :width])
    kt = cols(act_tm[..., width:2 * width])
    v = act_tm[..., 2 * width:].reshape(steps, nbo, seqs, heads, HEAD)
    v = v.transpose(1, 3, 2, 0, 4).reshape(nbo, heads, lanes, HEAD)
    beta = bg[:, :heads].reshape(-1)
    g = bg[:, heads:].reshape(-1)
    smem = pl.BlockSpec(memory_space=pltpu.SMEM)
    o, s_new = pl.pallas_call(
        functools.partial(_gdn_step_kernel, seqs=seqs, steps=steps, heads=heads),
        name="gdn_steps",
        out_shape=(jax.ShapeDtypeStruct((nbo, heads, lanes, HEAD), F32),
                   jax.ShapeDtypeStruct(s0.shape, F32)),
        grid=(nbo, heads),
        in_specs=[smem, smem,
                  pl.BlockSpec((None, None, HEAD, lanes), lambda i, h: (i, h, 0, 0)),
                  pl.BlockSpec((None, None, HEAD, lanes), lambda i, h: (i, h, 0, 0)),
                  pl.BlockSpec((None, None, lanes, HEAD), lambda i, h: (i, h, 0, 0)),
                  pl.BlockSpec((seqs, None, HEAD, HEAD), lambda i, h: (i, h, 0, 0))],
        out_specs=(pl.BlockSpec((None, None, lanes, HEAD), lambda i, h: (i, h, 0, 0)),
                   pl.BlockSpec((seqs, None, HEAD, HEAD), lambda i, h: (i, h, 0, 0))),
        compiler_params=_params("parallel", "parallel"),
    )(beta, g, qt, kt, v, s0)
    o = o.reshape(nbo, heads, seqs, steps, HEAD).transpose(0, 2, 3, 1, 4).reshape(b * steps, width)
    return o, s_new


def _gdn_out_gate(o, z, g):
    outs = []
    for h in range(o.shape[1] // HEAD):
        sl = slice(h * HEAD, (h + 1) * HEAD)
        outs.append(_rms(o[:, sl], g) * _silu(z[:, sl]))
    return jnp.concatenate(outs, axis=1)


def _mm_post_kernel(*refs, gated):
    if gated:
        a_ref, z_ref, ng_ref, w_ref, x_ref, gt_ref, pg_ref, o_ref = refs
        a = _gdn_out_gate(a_ref[...], z_ref[...], ng_ref[...]).astype(BF16)
    else:
        a_ref, w_ref, x_ref, gt_ref, pg_ref, o_ref = refs
        a = a_ref[...].astype(BF16)
    y = _dot(a, w_ref[...])
    o_ref[...] = x_ref[...] + gt_ref[...] * _rms(y, pg_ref[...])


def _matmul_post(a, w, x, gt, post_g, z=None, z_col=0, norm_g=None):
    m, k = a.shape
    d = w.shape[1]
    tm = _tile(m, 256)
    gated = z is not None
    in_specs = [pl.BlockSpec((tm, k), lambda i: (i, 0))]
    args = [a]
    if gated:
        in_specs += [pl.BlockSpec((tm, k), lambda i: (i, z_col)), pl.BlockSpec((1, HEAD), lambda i: (0, 0))]
        args += [z, norm_g.reshape(1, HEAD)]
    in_specs += [pl.BlockSpec((k, d), lambda i: (0, 0)),
                 pl.BlockSpec((tm, d), lambda i: (i, 0)),
                 _mod_spec(gt, tm, d),
                 pl.BlockSpec((1, d), lambda i: (0, 0))]
    args += [w, x, gt, post_g.reshape(1, d)]
    return pl.pallas_call(
        functools.partial(_mm_post_kernel, gated=gated),
        name="matmul_post",
        out_shape=jax.ShapeDtypeStruct((m, d), F32),
        grid=(m // tm,),
        in_specs=in_specs,
        out_specs=pl.BlockSpec((tm, d), lambda i: (i, 0)),
        compiler_params=_params("parallel"),
    )(*args)


def _ffn_kernel(x_ref, g_ref, sc_ref, sh_ref, wg_ref, wu_ref, wd_ref, gt_ref, pg_ref, o_ref, h_scr, acc):
    f = pl.program_id(1)

    @pl.when(f == 0)
    def _():
        h_scr[...] = _norm_mod(x_ref[...], g_ref[...], sc_ref[...], sh_ref[...]).astype(BF16)
        acc[...] = jnp.zeros_like(acc)

    h = h_scr[...]
    act = (_silu(_dot(h, wg_ref[...])) * _dot(h, wu_ref[...])).astype(BF16)
    acc[...] += _dot(act, wd_ref[...])

    @pl.when(f == pl.num_programs(1) - 1)
    def _():
        o_ref[...] = x_ref[...] + gt_ref[...] * _rms(acc[...], pg_ref[...])


def _dense_ffn(x, g, sc, sh, w_gu, w_down, gt, post_g, tm_pref=512, tf_pref=512):
    m, d = x.shape
    ff = w_down.shape[0]
    tm = _tile(m, tm_pref)
    tf = _tile(ff, tf_pref, LANES)
    nf = ff // tf
    return pl.pallas_call(
        _ffn_kernel,
        name="dense_ffn",
        out_shape=jax.ShapeDtypeStruct((m, d), F32),
        grid=(m // tm, nf),
        in_specs=[pl.BlockSpec((tm, d), lambda i, f: (i, 0)),
                  pl.BlockSpec((1, d), lambda i, f: (0, 0)),
                  _mod_spec(sc, tm, d), _mod_spec(sh, tm, d),
                  pl.BlockSpec((d, tf), lambda i, f: (0, f)),
                  pl.BlockSpec((d, tf), lambda i, f: (0, nf + f)),
                  pl.BlockSpec((tf, d), lambda i, f: (f, 0)),
                  _mod_spec(gt, tm, d),
                  pl.BlockSpec((1, d), lambda i, f: (0, 0))],
        out_specs=pl.BlockSpec((tm, d), lambda i, f: (i, 0)),
        scratch_shapes=[pltpu.VMEM((tm, d), BF16), pltpu.VMEM((tm, d), F32)],
        compiler_params=_params("parallel", "arbitrary"),
    )(x, g.reshape(1, d), sc, sh, w_gu, w_gu, w_down, gt, post_g.reshape(1, d))


def _router_kernel(x_ref, g_ref, sc_ref, sh_ref, wr_ref, h_ref, r_ref, *, experts):
    h = _norm_mod(x_ref[...], g_ref[...], sc_ref[...], sh_ref[...])
    h_ref[...] = h.astype(BF16)
    logits = _dot3(h, wr_ref[...])
    lane = lax.broadcasted_iota(jnp.int32, logits.shape, 1)
    lanef = lane.astype(F32)
    logits = jnp.where(lane < experts, logits, NEG_INF)
    m1 = jnp.max(logits, axis=-1, keepdims=True)
    i1 = jnp.min(jnp.where(logits == m1, lanef, float(LANES)), axis=-1, keepdims=True)
    rest = jnp.where(lanef == i1, NEG_INF, logits)
    m2 = jnp.max(rest, axis=-1, keepdims=True)
    i2 = jnp.min(jnp.where(rest == m2, lanef, float(LANES)), axis=-1, keepdims=True)
    e2 = jnp.exp(m2 - m1)
    w1 = 1.0 / (1.0 + e2)
    w2 = e2 / (1.0 + e2)
    r_ref[...] = jnp.where(lane == 0, i1, jnp.where(lane == 1, i2, jnp.where(lane == 2, w1, jnp.where(lane == 3, w2, 0.0))))


def _router(x, g, sc, sh, w_router):
    m, d = x.shape
    experts = w_router.shape[1]
    tm = _tile(m, 512)
    wr = jnp.pad(w_router, ((0, 0), (0, LANES - experts)))
    return pl.pallas_call(
        functools.partial(_router_kernel, experts=experts),
        name="moe_router",
        out_shape=(jax.ShapeDtypeStruct((m, d), BF16), jax.ShapeDtypeStruct((m, LANES), F32)),
        grid=(m // tm,),
        in_specs=[pl.BlockSpec((tm, d), lambda i: (i, 0)),
                  pl.BlockSpec((1, d), lambda i: (0, 0)),
                  _mod_spec(sc, tm, d), _mod_spec(sh, tm, d),
                  pl.BlockSpec((d, LANES), lambda i: (0, 0))],
        out_specs=(pl.BlockSpec((tm, d), lambda i: (i, 0)), pl.BlockSpec((tm, LANES), lambda i: (i, 0))),
        compiler_params=_params("parallel"),
    )(x, g.reshape(1, d), sc, sh, wr)


def _moe_ffn_kernel(te_ref, nu_ref, h_ref, wg_ref, wu_ref, wd_ref, o_ref, acc):
    i = pl.program_id(0)
    f = pl.program_id(1)
    last = pl.num_programs(1) - 1
    used = i < nu_ref[0]

    @pl.when(jnp.logical_and(used, f == 0))
    def _():
        acc[...] = jnp.zeros_like(acc)

    @pl.when(used)
    def _():
        h = h_ref[...]
        act = (_silu(_dot(h, wg_ref[...].astype(BF16))) * _dot(h, wu_ref[...].astype(BF16))).astype(BF16)
        acc[...] += _dot(act, wd_ref[...].astype(BF16))

    @pl.when(jnp.logical_and(used, f == last))
    def _():
        o_ref[...] = acc[...]

    @pl.when(jnp.logical_and(jnp.logical_not(used), f == last))
    def _():
        o_ref[...] = jnp.zeros_like(o_ref)


def _moe_ffn(hs, tile_expert, n_used, w_gu, w_down, tm, tf_pref=512):
    mp, d = hs.shape
    ff = w_down.shape[1]
    tf = _tile(ff, tf_pref, LANES)
    nf = ff // tf

    def fidx(i, f, nu):
        return jnp.where(i < nu[0], f, nf - 1)

    grid_spec = pltpu.PrefetchScalarGridSpec(
        num_scalar_prefetch=2,
        grid=(mp // tm, nf),
        in_specs=[pl.BlockSpec((tm, d), lambda i, f, te, nu: (i, 0), pipeline_mode=pl.Buffered(1)),
                  pl.BlockSpec((None, d, tf), lambda i, f, te, nu: (te[i], 0, fidx(i, f, nu))),
                  pl.BlockSpec((None, d, tf), lambda i, f, te, nu: (te[i], 0, nf + fidx(i, f, nu))),
                  pl.BlockSpec((None, tf, d), lambda i, f, te, nu: (te[i], fidx(i, f, nu), 0))],
        out_specs=pl.BlockSpec((tm, d), lambda i, f, te, nu: (i, 0), pipeline_mode=pl.Buffered(1)),
        scratch_shapes=[pltpu.VMEM((tm, d), F32)],
    )
    return pl.pallas_call(
        _moe_ffn_kernel,
        name="moe_ffn",
        out_shape=jax.ShapeDtypeStruct((mp, d), F32),
        grid_spec=grid_spec,
        compiler_params=_params("arbitrary", "arbitrary"),
    )(tile_expert, n_used, hs, w_gu, w_gu, w_down)


def _combine_kernel(y1_ref, y2_ref, r_ref, x_ref, gt_ref, pg_ref, o_ref):
    r = r_ref[...]
    y = r[:, 2:3] * y1_ref[...] + r[:, 3:4] * y2_ref[...]
    o_ref[...] = x_ref[...] + gt_ref[...] * _rms(y, pg_ref[...])


def _moe_combine(y1, y2, route, x, gt, post_g):
    m, d = x.shape
    tm = _tile(m, 256)
    return pl.pallas_call(
        _combine_kernel,
        name="moe_combine",
        out_shape=jax.ShapeDtypeStruct((m, d), F32),
        grid=(m // tm,),
        in_specs=[pl.BlockSpec((tm, d), lambda i: (i, 0)),
                  pl.BlockSpec((tm, d), lambda i: (i, 0)),
                  pl.BlockSpec((tm, LANES), lambda i: (i, 0)),
                  pl.BlockSpec((tm, d), lambda i: (i, 0)),
                  _mod_spec(gt, tm, d),
                  pl.BlockSpec((1, d), lambda i: (0, 0))],
        out_specs=pl.BlockSpec((tm, d), lambda i: (i, 0)),
        compiler_params=_params("parallel"),
    )(y1, y2, route, x, gt, post_g.reshape(1, d))


def _moe_layer(parts, g, post_g, w_router, w_gu, w_down, tm=768):
    experts = w_router.shape[1]
    routed = [_router(x, g, sc, sh, w_router) for x, sc, sh, _ in parts]
    h = jnp.concatenate([r[0] for r in routed], axis=0)
    route = jnp.concatenate([r[1] for r in routed], axis=0)
    m = h.shape[0]
    ids = route[:, :MOE_TOPK].astype(jnp.int32).reshape(-1)
    n_assign = m * MOE_TOPK
    counts = jnp.zeros((experts,), jnp.int32).at[ids].add(1)
    padded = ((counts + tm - 1) // tm) * tm
    starts_p = jnp.cumsum(padded) - padded
    starts = jnp.cumsum(counts) - counts
    order = jnp.argsort(ids, stable=True)
    sorted_ids = ids[order]
    dest_sorted = starts_p[sorted_ids] + (jnp.arange(n_assign, dtype=jnp.int32) - starts[sorted_ids])
    n_tiles = -(-(n_assign + experts * (tm - 1)) // tm)
    mp = n_tiles * tm
    src_tok = jnp.zeros((mp,), jnp.int32).at[dest_sorted].set((order // MOE_TOPK).astype(jnp.int32))
    dest = jnp.zeros((n_assign,), jnp.int32).at[order].set(dest_sorted.astype(jnp.int32))
    tile_start = jnp.arange(n_tiles, dtype=jnp.int32) * tm
    ends_p = jnp.cumsum(padded)
    tile_expert = jnp.minimum(jnp.searchsorted(ends_p, tile_start, side="right"), experts - 1).astype(jnp.int32)
    n_used = (ends_p[-1] // tm).astype(jnp.int32).reshape(1)
    last_e = tile_expert[jnp.maximum(n_used[0] - 1, 0)]
    tile_expert = jnp.where(jnp.arange(n_tiles) < n_used[0], tile_expert, last_e)
    hs = jnp.take(h, src_tok, axis=0, mode="clip")
    ys = _moe_ffn(hs, tile_expert, n_used, w_gu, w_down, tm)
    dest = dest.reshape(m, MOE_TOPK)
    outs = []
    row0 = 0
    for (x, _, _, gt), (_, route_x) in zip(parts, routed):
        rows = slice(row0, row0 + x.shape[0])
        row0 += x.shape[0]
        y1 = jnp.take(ys, dest[rows, 0], axis=0, mode="clip")
        y2 = jnp.take(ys, dest[rows, 1], axis=0, mode="clip")
        outs.append(_moe_combine(y1, y2, route_x, x, gt, post_g))
    return outs


def _block_mean_kernel(k_ref, o_ref):
    o_ref[0] = jnp.mean(k_ref[...], axis=0, keepdims=True)


def _block_means(k):
    t_len, width = k.shape
    nb = t_len // MOBA_BLOCK
    out = pl.pallas_call(
        _block_mean_kernel,
        name="moba_block_means",
        out_shape=jax.ShapeDtypeStruct((nb, 1, width), F32),
        grid=(nb,),
        in_specs=[pl.BlockSpec((MOBA_BLOCK, width), lambda j: (j, 0))],
        out_specs=pl.BlockSpec((1, 1, width), lambda j: (j, 0, 0)),
        compiler_params=_params("parallel"),
    )(k)
    return out.reshape(nb, width)


def _moba_seq_kernel(q_ref, k_ref, vt_ref, km_ref, o_ref, *, nb, hb):
    bs = MOBA_BLOCK
    ob = pl.program_id(1)
    own = pl.multiple_of(ob * bs, bs)
    causal = lax.broadcasted_iota(jnp.int32, (bs, bs), 0) <= lax.broadcasted_iota(jnp.int32, (bs, bs), 1)
    blk = lax.broadcasted_iota(jnp.int32, (bs, LANES), 1)
    blk_t = lax.broadcasted_iota(jnp.int32, (LANES, bs), 0)
    blkf_t = blk_t.astype(F32)
    hsl = [slice(h * HEAD, (h + 1) * HEAD) for h in range(hb)]

    qts = []
    for h in range(hb):
        q = q_ref[:, hsl[h]]
        gate = _dot3(km_ref[:, hsl[h]], q, _dot_nt)
        gate = jnp.where(blk_t < ob, gate, NEG_INF)
        sel = jnp.zeros((LANES, bs), F32)
        for _ in range(MOBA_TOPK):
            m = jnp.max(gate, axis=0, keepdims=True)
            first = jnp.min(jnp.where(gate == m, blkf_t, float(LANES)), axis=0, keepdims=True)
            pick = jnp.logical_and(blkf_t == first, m > NEG_INF)
            sel = jnp.where(pick, 1.0, sel)
            gate = jnp.where(pick, NEG_INF, gate)
        bias_t = jnp.where(sel > 0.0, 0.0, MASKED)
        q_t = (q * (HEAD ** -0.5 * math.log2(math.e))).T
        qts.append(jnp.concatenate([q_t, bias_t], axis=0).astype(BF16))

    def scores(h, start, hot):
        k_ext = jnp.concatenate([k_ref[pl.ds(start, bs), hsl[h]], hot], axis=1)
        return _dot(k_ext, qts[h])

    def one_hot(j):
        return jnp.where(blk == j, 1.0, 0.0).astype(BF16)

    no_hot = jnp.zeros((bs, LANES), BF16)
    ms, ls, accs = [], [], []
    for h in range(hb):
        s = jnp.where(causal, scores(h, own, no_hot), MASKED)
        m = jnp.max(s, axis=0, keepdims=True)
        p = jnp.exp2(s - m)
        ms.append(m)
        ls.append(jnp.sum(p, axis=0, keepdims=True))
        accs.append(_dot(vt_ref[ob, hsl[h], :], p.astype(BF16)))

    def body(j, carry):
        ss, ms, ls, accs = carry
        jn = jnp.minimum(j + 1, nb - 1)
        s_next = tuple(scores(h, pl.multiple_of(jn * bs, bs), one_hot(jn)) for h in range(hb))
        m_out, l_out, a_out = [], [], []
        for h in range(hb):
            m_new = jnp.maximum(ms[h], jnp.max(ss[h], axis=0, keepdims=True))
            alpha = jnp.exp2(ms[h] - m_new)
            p = jnp.exp2(ss[h] - m_new)
            m_out.append(m_new)
            l_out.append(alpha * ls[h] + jnp.sum(p, axis=0, keepdims=True))
            a_out.append(alpha * accs[h] + _dot(vt_ref[j, hsl[h], :], p.astype(BF16)))
        return s_next, tuple(m_out), tuple(l_out), tuple(a_out)

    s_first = tuple(scores(h, 0, one_hot(0)) for h in range(hb))
    _, _, ls, accs = lax.fori_loop(0, ob, body, (s_first, tuple(ms), tuple(ls), tuple(accs)))
    for h in range(hb):
        o_ref[:, hsl[h]] = (accs[h] / ls[h]).T.astype(o_ref.dtype)


def _moba_seq(q, k, v, heads, hb=4):
    t_len, width = q.shape
    bs = MOBA_BLOCK
    assert t_len % bs == 0 and heads % hb == 0
    nb = t_len // bs
    assert nb <= LANES
    kmean = jnp.pad(_block_means(k), ((0, LANES - nb), (0, 0)))
    kb = k.astype(BF16)
    vt = v.astype(BF16).reshape(nb, bs, width).transpose(0, 2, 1)
    return pl.pallas_call(
        functools.partial(_moba_seq_kernel, nb=nb, hb=hb),
        name="moba_seq",
        out_shape=jax.ShapeDtypeStruct((t_len, width), BF16),
        grid=(heads // hb, nb),
        in_specs=[pl.BlockSpec((bs, hb * HEAD), lambda h, i: (i, h)),
                  pl.BlockSpec((t_len, hb * HEAD), lambda h, i: (0, h)),
                  pl.BlockSpec((nb, hb * HEAD, bs), lambda h, i: (0, h, 0)),
                  pl.BlockSpec((LANES, hb * HEAD), lambda h, i: (0, h))],
        out_specs=pl.BlockSpec((bs, hb * HEAD), lambda h, i: (i, h)),
        compiler_params=_params("parallel", "arbitrary"),
    )(q, kb, vt, kmean)


def _head_match_bias(n_rows, heads, cols_used):
    row = lax.broadcasted_iota(jnp.int32, (n_rows, LANES), 0)
    col = lax.broadcasted_iota(jnp.int32, (n_rows, LANES), 1)
    keep = jnp.logical_and(row % heads == col % heads, col < cols_used)
    return keep, row, col


def _moba_paged_kernel(pt_ref, qt_ref, kn_ref, vn_ref, *refs, heads, steps, nblk, page, bpb):
    k_refs = refs[:2 * bpb]
    v_refs = refs[2 * bpb:4 * bpb]
    o_ref, qtb, bias, acc, ks, st_m, st_l, st_w = refs[4 * bpb:]
    j = pl.program_id(1)
    cols = steps * heads
    n_rows = page * heads

    @pl.when(j == 0)
    def _():
        qtb[...] = (qt_ref[0] * (HEAD ** -0.5 * math.log2(math.e))).astype(BF16)
        keep, _, _ = _head_match_bias(n_rows, heads, cols)
        bias[...] = jnp.where(keep, 0.0, MASKED)

    pages = range(2 * bpb)
    s = [_dot(k_refs[p][0].astype(BF16), qtb[...]) + bias[...] for p in pages]
    pmax = [jnp.max(s[p], axis=0, keepdims=True) for p in pages]
    m = [jnp.maximum(pmax[2 * u], pmax[2 * u + 1]) for u in range(bpb)]
    prob = [jnp.exp2(s[p] - m[p // 2]) for p in pages]
    psum = [jnp.sum(prob[p], axis=0, keepdims=True) for p in pages]
    pv = [_dot(prob[p].T.astype(BF16), v_refs[p][0].astype(BF16)) for p in pages]
    ksum = [jnp.sum(k_refs[p][0].reshape(page, heads, HEAD), axis=0) for p in pages]
    for u in range(bpb):
        blk = j * bpb + u
        acc[blk] = pv[2 * u] + pv[2 * u + 1]
        ks[blk] = ksum[2 * u] + ksum[2 * u + 1]
        st_m[pl.ds(blk, 1), :] = m[u]
        st_l[pl.ds(blk, 1), :] = psum[2 * u] + psum[2 * u + 1]

    @pl.when(j == nblk // bpb - 1)
    def _():
        qt = qt_ref[0]
        hrow = lax.broadcasted_iota(jnp.int32, (heads, LANES), 0)
        hcol = lax.broadcasted_iota(jnp.int32, (heads, LANES), 1) % heads
        gates = []
        for b in range(nblk):
            g_all = _dot3(ks[b], qt) * (1.0 / (2 * page))
            gates.append(jnp.sum(jnp.where(hrow == hcol, g_all, 0.0), axis=0, keepdims=True))
        sel = []
        for b in range(nblk):
            rank = jnp.zeros((1, LANES), F32)
            for o in range(nblk):
                if o == b:
                    continue
                ahead = gates[o] > gates[b]
                if o < b:
                    ahead = jnp.logical_or(ahead, gates[o] == gates[b])
                rank = rank + jnp.where(ahead, 1.0, 0.0)
            sel.append(rank < float(MOBA_TOPK))
        keep, row, col = _head_match_bias(cols, heads, cols)
        keep = jnp.logical_and(keep, row // heads <= col // heads)
        s_own = _dot(kn_ref[0].astype(BF16), qtb[...]) + jnp.where(keep, 0.0, MASKED)
        m_own = jnp.max(s_own, axis=0, keepdims=True)
        p_own = jnp.exp2(s_own - m_own)
        l_own = jnp.sum(p_own, axis=0, keepdims=True)
        v_own = vn_ref[0]
        if cols < LANES:
            p_own = jnp.concatenate([p_own, jnp.zeros((LANES - cols, LANES), F32)], axis=0)
            v_own = jnp.concatenate([v_own, jnp.zeros((LANES - cols, HEAD), F32)], axis=0)
        acc_own = _dot(p_own.T.astype(BF16), v_own.astype(BF16))
        m_all = m_own
        for b in range(nblk):
            m_all = jnp.maximum(m_all, jnp.where(sel[b], st_m[b:b + 1, :], MASKED))
        w_own = jnp.exp2(m_own - m_all)
        den = w_own * l_own
        ws = []
        for b in range(nblk):
            wb = jnp.where(sel[b], jnp.exp2(st_m[b:b + 1, :] - m_all), 0.0)
            den = den + wb * st_l[b:b + 1, :]
            ws.append(wb)
        inv = 1.0 / den
        st_w[...] = jnp.zeros_like(st_w)
        for b in range(nblk):
            st_w[b:b + 1, :] = ws[b] * inv
        st_w[nblk:nblk + 1, :] = w_own * inv
        eye_r = lax.broadcasted_iota(jnp.int32, (LANES, LANES), 0)
        eye_c = lax.broadcasted_iota(jnp.int32, (LANES, LANES), 1)
        eye = (eye_r == eye_c).astype(BF16)
        wr = st_w[...]
        w_hi, w_mid = _split(wr)
        w_lo = (wr - w_hi.astype(F32) - w_mid.astype(F32)).astype(BF16)
        wcol = _dot_nt(eye, w_hi) + (_dot_nt(eye, w_mid) + _dot_nt(eye, w_lo))
        out = wcol[:, nblk:nblk + 1] * acc_own
        for b in range(nblk):
            out = out + wcol[:, b:b + 1] * acc[b]
        o_ref[0] = out[:cols, :]


def _moba_paged(q, k_new, v_new, cache_k, cache_v, page_table, heads):
    b, steps, width = q.shape
    n_phys, page = cache_k.shape[0], cache_k.shape[1]
    n_pages = page_table.shape[1]
    assert MOBA_BLOCK == 2 * page and n_pages % 2 == 0 and steps <= MOBA_BLOCK
    nblk = n_pages // 2
    cols = steps * heads
    n_rows = page * heads
    stat_rows = -(-(nblk + 1) // 8) * 8
    assert cols <= LANES
    ck = cache_k.reshape(n_phys, n_rows, HEAD)
    cv = cache_v.reshape(n_phys, n_rows, HEAD)
    pt = page_table.reshape(-1).astype(jnp.int32)
    qt = jnp.pad(q.reshape(b, cols, HEAD).transpose(0, 2, 1), ((0, 0), (0, 0), (0, LANES - cols)))
    kn = k_new.reshape(b, cols, HEAD)
    vn = v_new.reshape(b, cols, HEAD)

    bpb = 2 if nblk % 2 == 0 else 1
    ppb = 2 * bpb

    def page_spec(off):
        return pl.BlockSpec((1, n_rows, HEAD), lambda i, j, pt: (pt[i * n_pages + ppb * j + off], 0, 0))

    page_specs = [page_spec(off) for off in range(ppb)]
    new_spec = pl.BlockSpec((1, cols, HEAD), lambda i, j, pt: (i, 0, 0))
    grid_spec = pltpu.PrefetchScalarGridSpec(
        num_scalar_prefetch=1,
        grid=(b, nblk // bpb),
        in_specs=[pl.BlockSpec((1, HEAD, LANES), lambda i, j, pt: (i, 0, 0)), new_spec, new_spec]
        + page_specs + page_specs,
        out_specs=pl.BlockSpec((1, cols, HEAD), lambda i, j, pt: (i, 0, 0)),
        scratch_shapes=[pltpu.VMEM((HEAD, LANES), BF16), pltpu.VMEM((n_rows, LANES), F32),
                        pltpu.VMEM((nblk, LANES, HEAD), F32), pltpu.VMEM((nblk, heads, HEAD), F32),
                        pltpu.VMEM((stat_rows, LANES), F32), pltpu.VMEM((stat_rows, LANES), F32),
                        pltpu.VMEM((stat_rows, LANES), F32)],
    )
    out = pl.pallas_call(
        functools.partial(_moba_paged_kernel, heads=heads, steps=steps, nblk=nblk, page=page, bpb=bpb),
        name="moba_paged",
        out_shape=jax.ShapeDtypeStruct((b, cols, HEAD), F32),
        grid_spec=grid_spec,
        compiler_params=_params("parallel", "arbitrary"),
    )(pt, qt, kn, vn, *([ck] * ppb), *([cv] * ppb))
    return out.reshape(b, steps, width)


def _rope_tables(pos):
    half = HEAD // 2
    inv = ROPE_THETA ** (-jnp.arange(half, dtype=F32) / half)
    ang = pos.astype(F32)[:, None] * inv[None, :]
    cos = jnp.cos(ang)
    sin = jnp.sin(ang)
    return jnp.concatenate([cos, cos], axis=1), jnp.concatenate([-sin, sin], axis=1)


def _chunks(mod, rep):
    parts = jnp.split(mod, 6, axis=-1)
    return [jnp.repeat(p, rep, axis=0) if rep > 1 else p for p in parts]


def kernel(x_prompt, x_sample, cache_k, cache_v, page_table, state_delta, state_conv, c_prompt, c_sample, ada_w, ada_b, ln_mix_pre, ln_mix_post, ln_ffn_pre, ln_ffn_post, a_w_in, a_conv_w, a_A_log, a_dt_bias, a_norm_g, a_w_out, kv_ada_w, kv_ada_b, kv_norm_g, w_kv, b_w_q, b_w_out, f_w_gu, f_w_down, m_router, m_w_gu, m_w_down):
    bp, t_p, d = x_prompt.shape
    bs, t_s, _ = x_sample.shape
    assert bp == 1
    gh = a_A_log.shape[1]
    ah = cache_k.shape[2]
    gw = gh * HEAD
    aw = ah * HEAD
    past = page_table.shape[1] * cache_k.shape[1]

    c_all = jnp.concatenate([c_prompt, c_sample], axis=0)
    r_all = c_all.shape[0]
    c_all = jnp.pad(c_all, ((0, (-r_all) % 8), (0, 0)))
    mods = [_modulation(c_all, ada_w, ada_b, l) for l in range(2)]
    kv_mod = _modulation(c_all, kv_ada_w[None], kv_ada_b[None])

    def mod_p(m):
        return m[0:1]

    def mod_s(m):
        return m[1:1 + bs]

    xp = x_prompt.reshape(t_p, d)
    xs = x_sample.reshape(bs * t_s, d)

    w_in = a_w_in[0]
    n_conv = 3 * gw
    w_ba = w_in[:, n_conv + gw:]
    w_out0 = a_w_out[0].astype(BF16)

    sh_m, sc_m, gt_m, sh_f, sc_f, gt_f = _chunks(mod_p(mods[0]), 1)
    proj_p = _norm_mod_matmul(xp, ln_mix_pre[0], sc_m, sh_m, w_in, n=n_conv + gw)
    bg_p = _gdn_gates(xp, ln_mix_pre[0], sc_m, sh_m, w_ba, a_A_log[0], a_dt_bias[0])
    act_p = _gdn_conv_seq(proj_p, a_conv_w[0], jnp.zeros((CONV_W - 1, n_conv), F32), gw)
    o_p, delta_p = _gdn_chunked(act_p, bg_p, jnp.zeros((gh, HEAD, HEAD), F32), gh)
    conv_p = proj_p[t_p - (CONV_W - 1):, :n_conv]
    xp = _matmul_post(o_p, w_out0, xp, gt_m, ln_mix_post[0], z=proj_p, z_col=3, norm_g=a_norm_g[0])

    ssh_m, ssc_m, sgt_m, ssh_f, ssc_f, sgt_f = _chunks(mod_s(mods[0]), t_s)
    proj_s = _norm_mod_matmul(xs, ln_mix_pre[0], ssc_m, ssh_m, w_in, n=n_conv + gw)
    bg_s = _gdn_gates(xs, ln_mix_pre[0], ssc_m, ssh_m, w_ba, a_A_log[0], a_dt_bias[0])
    qkv_s = proj_s[:, :n_conv].reshape(bs, t_s, n_conv)
    xc_s = jnp.concatenate([state_conv[:, 0].transpose(1, 0, 2), qkv_s.transpose(1, 0, 2)], axis=0)
    act_s = _gdn_conv_steps(xc_s, a_conv_w[0], gw)
    o_s, delta_s = _gdn_steps(act_s, bg_s, state_delta[:, 0], gh)
    conv_s = qkv_s[:, t_s - (CONV_W - 1):]
    xs = _matmul_post(o_s, w_out0, xs, sgt_m, ln_mix_post[0], z=proj_s, z_col=3, norm_g=a_norm_g[0])

    w_gu0 = f_w_gu[0].astype(BF16)
    w_dn0 = f_w_down[0].astype(BF16)
    xp = _dense_ffn(xp, ln_ffn_pre[0], sc_f, sh_f, w_gu0, w_dn0, gt_f, ln_ffn_post[0])
    xs = _dense_ffn(xs, ln_ffn_pre[0], ssc_f, ssh_f, w_gu0, w_dn0, sgt_f, ln_ffn_post[0])

    cos_p, sin_p = _rope_tables(jnp.arange(t_p))
    cos_s, sin_s = _rope_tables(jnp.tile(past + jnp.arange(t_s), bs))
    ksh, ksc = jnp.split(kv_mod, 2, axis=-1)
    ksc_s = jnp.repeat(mod_s(ksc), t_s, axis=0)
    ksh_s = jnp.repeat(mod_s(ksh), t_s, axis=0)
    k_p = _norm_mod_matmul(xp, kv_norm_g, mod_p(ksc), mod_p(ksh), w_kv, 0, aw, cos_p, sin_p, rope=True)
    v_p = _norm_mod_matmul(xp, kv_norm_g, mod_p(ksc), mod_p(ksh), w_kv, aw, aw)
    k_s = _norm_mod_matmul(xs, kv_norm_g, ksc_s, ksh_s, w_kv, 0, aw, cos_s, sin_s, rope=True)
    v_s = _norm_mod_matmul(xs, kv_norm_g, ksc_s, ksh_s, w_kv, aw, aw)

    w_out1 = b_w_out[0].astype(BF16)
    sh_m, sc_m, gt_m, sh_f, sc_f, gt_f = _chunks(mod_p(mods[1]), 1)
    ssh_m, ssc_m, sgt_m, ssh_f, ssc_f, sgt_f = _chunks(mod_s(mods[1]), t_s)
    q_p = _norm_mod_matmul(xp, ln_mix_pre[1], sc_m, sh_m, b_w_q[0], cos=cos_p, sin=sin_p, rope=True)
    q_s = _norm_mod_matmul(xs, ln_mix_pre[1], ssc_m, ssh_m, b_w_q[0], cos=cos_s, sin=sin_s, rope=True)
    att_p = _moba_seq(q_p, k_p, v_p, ah)
    att_s = _moba_paged(q_s.reshape(bs, t_s, aw), k_s.reshape(bs, t_s, aw), v_s.reshape(bs, t_s, aw),
                        cache_k, cache_v, page_table, ah)
    xp = _matmul_post(att_p, w_out1, xp, gt_m, ln_mix_post[1])
    xs = _matmul_post(att_s.reshape(bs * t_s, aw), w_out1, xs, sgt_m, ln_mix_post[1])

    y_p, y_s = _moe_layer([(xp, sc_f, sh_f, gt_f), (xs, ssc_f, ssh_f, sgt_f)], ln_ffn_pre[1], ln_ffn_post[1],
                          m_router[0], m_w_gu[0], m_w_down[0])

    return (y_p.reshape(1, t_p, d), y_s.reshape(bs, t_s, d),
            delta_p.reshape(1, 1, gh, HEAD, HEAD), conv_p.reshape(1, 1, CONV_W - 1, n_conv),
            k_p.reshape(1, t_p, ah, HEAD), v_p.reshape(1, t_p, ah, HEAD),
            delta_s.reshape(bs, 1, gh, HEAD, HEAD), conv_s.reshape(bs, 1, CONV_W - 1, n_conv),
            k_s.reshape(bs, t_s, ah, HEAD), v_s.reshape(bs, t_s, ah, HEAD))
```

```python
import functools
import math

import jax
import jax.numpy as jnp
from jax import lax
from jax.experimental import pallas as pl
from jax.experimental.pallas import tpu as pltpu

F32 = jnp.float32
BF16 = jnp.bfloat16
NEG_INF = float("-inf")
MASKED = -1e30

EPS = 1e-6
HEAD = 128
GDN_CHUNK = 64
CONV_W = 4
MOBA_BLOCK = 256
MOBA_TOPK = 3
ROPE_THETA = 10000.0
MOE_TOPK = 2
LANES = 128
VMEM_LIMIT = 52 * 1024 * 1024


def _params(*sem):
    return pltpu.CompilerParams(dimension_semantics=sem, vmem_limit_bytes=VMEM_LIMIT)


def _tile(n, pref, mult=8):
    if n <= pref:
        return n
    t = (pref // mult) * mult
    while t >= mult:
        if n % t == 0:
            return t
        t -= mult
    return n


def _dot(a, b):
    return jnp.dot(a, b, preferred_element_type=F32)


def _dot_nt(a, b):
    return lax.dot_general(a, b, (((1,), (1,)), ((), ())), preferred_element_type=F32)


def _dot_tn(a, b):
    return lax.dot_general(a, b, (((0,), (0,)), ((), ())), preferred_element_type=F32)


def _split(a):
    hi = a.astype(BF16)
    lo = (a - hi.astype(F32)).astype(BF16)
    return hi, lo


def _dot3(a, b, dot=_dot):
    ah, al = _split(a)
    bh, bl = _split(b)
    return dot(ah, bh) + (dot(ah, bl) + dot(al, bh))


def _dot_ones(ones, a, left):
    hi, mid = _split(a)
    lo = (a - hi.astype(F32) - mid.astype(F32)).astype(BF16)
    if left:
        return _dot(ones, hi) + (_dot(ones, mid) + _dot(ones, lo))
    return _dot(hi, ones) + (_dot(mid, ones) + _dot(lo, ones))


def _sigmoid(x):
    return 1.0 / (1.0 + jnp.exp(-x))


def _silu(x):
    return x * _sigmoid(x)


def _softplus(x):
    return jnp.maximum(x, 0.0) + jnp.log(1.0 + jnp.exp(-jnp.abs(x)))


def _rms(x, g):
    return x * lax.rsqrt(jnp.mean(x * x, axis=-1, keepdims=True) + EPS) * g


def _norm_mod(x, g, sc, sh):
    return _rms(x, g) * (1.0 + sc) + sh


def _mod_spec(mod, tm, k):
    if mod.shape[0] == 1:
        return pl.BlockSpec((1, k), lambda i, *_: (0, 0))
    return pl.BlockSpec((tm, k), lambda i, *_: (i, 0))


def _mod_kernel(c_ref, w_ref, b_ref, o_ref):
    o_ref[...] = _dot3(_silu(c_ref[...]), w_ref[...]) + b_ref[...]


def _modulation(c, w, b, layer=0):
    r, d = c.shape
    n = w.shape[2]
    tn = _tile(n, 512, LANES)
    return pl.pallas_call(
        _mod_kernel,
        name="adaln_modulation",
        out_shape=jax.ShapeDtypeStruct((r, n), F32),
        grid=(n // tn,),
        in_specs=[pl.BlockSpec((r, d), lambda j: (0, 0)),
                  pl.BlockSpec((None, d, tn), lambda j: (layer, 0, j)),
                  pl.BlockSpec((None, 1, tn), lambda j: (layer, 0, j))],
        out_specs=pl.BlockSpec((r, tn), lambda j: (0, j)),
        compiler_params=_params("parallel"),
    )(c, w, b.reshape(b.shape[0], 1, n))


def _rope_tile(y, cos, sin):
    outs = []
    for h in range(y.shape[1] // HEAD):
        yh = y[:, h * HEAD:(h + 1) * HEAD]
        outs.append(yh * cos + pltpu.roll(yh, HEAD // 2, 1) * sin)
    return outs[0] if len(outs) == 1 else jnp.concatenate(outs, axis=1)


def _nm_mm_kernel(x_ref, g_ref, sc_ref, sh_ref, w_ref, cos_ref, sin_ref, o_ref, *rest, rope, copy):
    j = pl.program_id(1)
    h_scr = rest[-1]

    @pl.when(j == 0)
    def _():
        h_scr[...] = _norm_mod(x_ref[...], g_ref[...], sc_ref[...], sh_ref[...]).astype(BF16)

    y = _dot(h_scr[...], w_ref[...].astype(BF16))
    if rope:
        y = _rope_tile(y, cos_ref[...], sin_ref[...])
    o_ref[...] = y
    if copy == "bf16":
        rest[0][...] = y.astype(BF16)
    elif copy == "bf16_blocks_t":
        for c in range(y.shape[0] // MOBA_BLOCK):
            rest[0][c] = y[c * MOBA_BLOCK:(c + 1) * MOBA_BLOCK, :].T.astype(BF16)


def _norm_mod_matmul(x, g, sc, sh, w, col0=0, n=None, cos=None, sin=None, rope=False, copy=None,
                     tm_pref=1024, tn_pref=512):
    m, k = x.shape
    n = w.shape[1] - col0 if n is None else n
    tm = _tile(m, tm_pref)
    tn = _tile(math.gcd(n, col0) if col0 else n, tn_pref, LANES)
    if cos is None:
        cos = jnp.zeros((m, HEAD), F32)
        sin = cos
    assert col0 % tn == 0 and n % tn == 0
    c0 = col0 // tn
    out_shape = jax.ShapeDtypeStruct((m, n), F32)
    out_specs = pl.BlockSpec((tm, tn), lambda i, j: (i, j))
    if copy == "bf16":
        out_shape = (out_shape, jax.ShapeDtypeStruct((m, n), BF16))
        out_specs = (out_specs, pl.BlockSpec((tm, tn), lambda i, j: (i, j)))
    elif copy == "bf16_blocks_t":
        assert tm % MOBA_BLOCK == 0
        out_shape = (out_shape, jax.ShapeDtypeStruct((m // MOBA_BLOCK, n, MOBA_BLOCK), BF16))
        out_specs = (out_specs, pl.BlockSpec((tm // MOBA_BLOCK, tn, MOBA_BLOCK), lambda i, j: (i, j, 0)))
    return pl.pallas_call(
        functools.partial(_nm_mm_kernel, rope=rope, copy=copy),
        name="norm_mod_matmul",
        out_shape=out_shape,
        grid=(m // tm, n // tn),
        in_specs=[pl.BlockSpec((tm, k), lambda i, j: (i, 0)),
                  pl.BlockSpec((1, k), lambda i, j: (0, 0)),
                  _mod_spec(sc, tm, k), _mod_spec(sh, tm, k),
                  pl.BlockSpec((k, tn), lambda i, j: (0, c0 + j)),
                  pl.BlockSpec((tm, HEAD), lambda i, j: (i, 0)),
                  pl.BlockSpec((tm, HEAD), lambda i, j: (i, 0))],
        out_specs=out_specs,
        scratch_shapes=[pltpu.VMEM((tm, k), BF16)],
        compiler_params=_params("parallel", "arbitrary"),
    )(x, g.reshape(1, k), sc, sh, w, cos, sin)


def _gates_kernel(x_ref, g_ref, sc_ref, sh_ref, w_ref, alog_ref, dtb_ref, o_ref, *, heads):
    h = _norm_mod(x_ref[...], g_ref[...], sc_ref[...], sh_ref[...])
    ba = _dot3(h, w_ref[...])
    lane = lax.broadcasted_iota(jnp.int32, ba.shape, 1)
    decay = -jnp.exp(alog_ref[...]) * _softplus(ba + dtb_ref[...])
    o_ref[...] = jnp.where(lane < heads, _sigmoid(ba), decay)


def _gdn_gates(x, g, sc, sh, w_ba, a_log, dt_bias):
    m, k = x.shape
    heads = a_log.shape[0]
    tm = _tile(m, 512)
    pad = jnp.zeros((heads,), F32)
    alog2 = jnp.concatenate([pad, a_log]).reshape(1, 2 * heads)
    dtb2 = jnp.concatenate([pad, dt_bias]).reshape(1, 2 * heads)
    return pl.pallas_call(
        functools.partial(_gates_kernel, heads=heads),
        name="gdn_gates",
        out_shape=jax.ShapeDtypeStruct((m, 2 * heads), F32),
        grid=(m // tm,),
        in_specs=[pl.BlockSpec((tm, k), lambda i: (i, 0)),
                  pl.BlockSpec((1, k), lambda i: (0, 0)),
                  _mod_spec(sc, tm, k), _mod_spec(sh, tm, k),
                  pl.BlockSpec((k, 2 * heads), lambda i: (0, 0)),
                  pl.BlockSpec((1, 2 * heads), lambda i: (0, 0)),
                  pl.BlockSpec((1, 2 * heads), lambda i: (0, 0))],
        out_specs=pl.BlockSpec((tm, 2 * heads), lambda i: (i, 0)),
        compiler_params=_params("parallel"),
    )(x, g.reshape(1, k), sc, sh, w_ba, alog2, dtb2)


def _qkv_act(acc, part):
    act = _silu(acc)
    outs = []
    for h in range(act.shape[1] // HEAD):
        a = act[:, h * HEAD:(h + 1) * HEAD]
        inv = lax.rsqrt(jnp.sum(a * a, axis=-1, keepdims=True) + EPS)
        fac = jnp.where(part == 0, inv * (HEAD ** -0.5), jnp.where(part == 1, inv, 1.0))
        outs.append(a * fac)
    return jnp.concatenate(outs, axis=1)


def _conv_seq_kernel(x_ref, w_ref, b0_ref, o_ref, xs, *, tt):
    part = pl.program_id(0)
    t = pl.program_id(1)

    @pl.when(t == 0)
    def _():
        xs[0:8, :] = b0_ref[...]

    @pl.when(t > 0)
    def _():
        xs[0:8, :] = xs[tt:tt + 8, :]

    xs[8:8 + tt, :] = x_ref[...]
    w = w_ref[...]
    acc = xs[5:5 + tt, :] * w[0:1, :]
    for j in range(1, CONV_W):
        acc = acc + xs[5 + j:5 + j + tt, :] * w[j:j + 1, :]
    o_ref[...] = _qkv_act(acc, part)


def _gdn_conv_seq(qkvz, conv_w, buf, width):
    t_len = qkvz.shape[0]
    tt = _tile(t_len, 512)
    b0 = jnp.concatenate([jnp.zeros((8 - (CONV_W - 1), 3 * width), F32), buf], axis=0)
    return pl.pallas_call(
        functools.partial(_conv_seq_kernel, tt=tt),
        name="gdn_conv_seq",
        out_shape=jax.ShapeDtypeStruct((t_len, 3 * width), F32),
        grid=(3, t_len // tt),
        in_specs=[pl.BlockSpec((tt, width), lambda c, t: (t, c)),
                  pl.BlockSpec((CONV_W, width), lambda c, t: (0, c)),
                  pl.BlockSpec((8, width), lambda c, t: (0, c))],
        out_specs=pl.BlockSpec((tt, width), lambda c, t: (t, c)),
        scratch_shapes=[pltpu.VMEM((tt + 8, width), F32)],
        compiler_params=_params("parallel", "arbitrary"),
    )(qkvz, conv_w, b0)


def _conv_step_kernel(x_ref, w_ref, o_ref, *, steps):
    part = pl.program_id(0)
    w = w_ref[...]
    for t in range(steps):
        acc = x_ref[t] * w[0:1, :]
        for j in range(1, CONV_W):
            acc = acc + x_ref[t + j] * w[j:j + 1, :]
        o_ref[t] = _qkv_act(acc, part)


def _gdn_conv_steps(xc, conv_w, width):
    rows, b, _ = xc.shape
    steps = rows - (CONV_W - 1)
    return pl.pallas_call(
        functools.partial(_conv_step_kernel, steps=steps),
        name="gdn_conv_steps",
        out_shape=jax.ShapeDtypeStruct((steps, b, 3 * width), F32),
        grid=(3,),
        in_specs=[pl.BlockSpec((rows, b, width), lambda c: (0, 0, c)),
                  pl.BlockSpec((CONV_W, width), lambda c: (0, c))],
        out_specs=pl.BlockSpec((steps, b, width), lambda c: (0, 0, c)),
        compiler_params=_params("parallel"),
    )(xc, conv_w)


def _unit_lower_inverses(mats, c):
    row = lax.broadcasted_iota(jnp.int32, (c, c), 0)
    col = lax.broadcasted_iota(jnp.int32, (c, c), 1)
    eye = (row == col).astype(F32)
    base = min(c, 16)
    same = (row // base) == (col // base)
    xs = [jnp.where(same, a, 0.0) for a in mats]
    ts = [eye - x for x in xs]
    for _ in range(int(math.log2(base)) - 1):
        xs = [_dot3(x, x) for x in xs]
        ts = [t + _dot3(t, x) for t, x in zip(ts, xs)]
    size = base
    while size < c:
        inner = same
        size *= 2
        same = (row // size) == (col // size)
        pick = jnp.logical_and(same, jnp.logical_not(inner))
        ys = [_dot3(jnp.where(pick, a, 0.0), t) for a, t in zip(mats, ts)]
        ts = [t - _dot3(t, y) for t, y in zip(ts, ys)]
    return ts


def _gdn_chunk_kernel(q_ref, k_ref, v_ref, bg_ref, gt_ref, s0_ref, o_ref, s_out_ref, s_scr, *, heads, c):
    n = pl.program_id(0)

    @pl.when(n == 0)
    def _():
        s_scr[...] = s0_ref[...]

    row = lax.broadcasted_iota(jnp.int32, (c, c), 0)
    col = lax.broadcasted_iota(jnp.int32, (c, c), 1)
    lower = row >= col
    strict = row > col
    bg = bg_ref[...]
    gcum = _dot_ones(lower.astype(BF16), bg, left=True)
    gcum_t = _dot_ones((row <= col).astype(BF16), gt_ref[0], left=False)

    hs = range(heads)
    sls = [slice(h * HEAD, (h + 1) * HEAD) for h in hs]
    q = [q_ref[:, sl] for sl in sls]
    k = [k_ref[:, sl] for sl in sls]
    gi = [gcum[:, heads + h:heads + h + 1] for h in hs]
    decay = [jnp.exp(jnp.where(lower, gi[h] - gcum_t[h:h + 1, :], NEG_INF)) for h in hs]
    kb = [k[h] * bg[:, h:h + 1] for h in hs]
    egi = [jnp.exp(g) for g in gi]
    qk_kk = [_dot3(jnp.concatenate([q[h], kb[h]], axis=0), k[h], _dot_nt) for h in hs]
    qk = [qk_kk[h][:c] * decay[h] for h in hs]
    ts = _unit_lower_inverses([jnp.where(strict, qk_kk[h][c:] * decay[h], 0.0) for h in hs], c)
    sol = [_dot3(ts[h], jnp.concatenate([v_ref[:, sls[h]] * bg[:, h:h + 1], kb[h] * egi[h]], axis=1))
           for h in hs]
    s = [s_scr[h] for h in hs]
    ws_qs = [_dot3(jnp.concatenate([sol[h][:, HEAD:], q[h] * egi[h]], axis=0), s[h]) for h in hs]
    v_new = [sol[h][:, :HEAD] - ws_qs[h][:c] for h in hs]
    o = [ws_qs[h][c:] + _dot3(qk[h], v_new[h]) for h in hs]
    for h in hs:
        o_ref[:, sls[h]] = o[h]
    g_last = [gi[h][c - 1:c, :] for h in hs]
    kd = [k[h] * jnp.exp(g_last[h] - gi[h]) for h in hs]
    s_new = [s[h] * jnp.exp(g_last[h]) + _dot3(kd[h], v_new[h], _dot_tn) for h in hs]
    for h in hs:
        s_scr[h] = s_new[h]

    @pl.when(n == pl.num_programs(0) - 1)
    def _():
        s_out_ref[...] = s_scr[...]


def _gdn_chunked(act, bg, s0, heads):
    t_len = act.shape[0]
    c = GDN_CHUNK
    assert t_len % c == 0
    width = heads * HEAD
    n_chunks = t_len // c
    g_t = bg[:, heads:].reshape(n_chunks, c, heads).transpose(0, 2, 1)
    return pl.pallas_call(
        functools.partial(_gdn_chunk_kernel, heads=heads, c=c),
        name="gdn_chunked",
        out_shape=(jax.ShapeDtypeStruct((t_len, width), F32),
                   jax.ShapeDtypeStruct((heads, HEAD, HEAD), F32)),
        grid=(n_chunks,),
        in_specs=[pl.BlockSpec((c, width), lambda n: (n, 0)),
                  pl.BlockSpec((c, width), lambda n: (n, 1)),
                  pl.BlockSpec((c, width), lambda n: (n, 2)),
                  pl.BlockSpec((c, 2 * heads), lambda n: (n, 0)),
                  pl.BlockSpec((1, heads, c), lambda n: (n, 0, 0)),
                  pl.BlockSpec((heads, HEAD, HEAD), lambda n: (0, 0, 0))],
        out_specs=(pl.BlockSpec((c, width), lambda n: (n, 0)),
                   pl.BlockSpec((heads, HEAD, HEAD), lambda n: (0, 0, 0))),
        scratch_shapes=[pltpu.VMEM((heads, HEAD, HEAD), F32)],
        compiler_params=_params("arbitrary"),
    )(act, act, act, bg, g_t, s0)


def _gdn_step_kernel(beta_ref, g_ref, qt_ref, kt_ref, v_ref, s0_ref, o_ref, s_ref, *, seqs, steps, heads):
    bo = pl.program_id(0)
    h = pl.program_id(1)
    for bl in range(seqs):
        s = s0_ref[bl]
        for t in range(steps):
            colx = bl * steps + t
            idx = ((bo * seqs + bl) * steps + t) * heads + h
            beta = beta_ref[idx]
            a = jnp.exp(jnp.full((1, HEAD), g_ref[idx], F32))
            kc = jnp.broadcast_to(kt_ref[:, colx:colx + 1], (HEAD, HEAD))
            qc = jnp.broadcast_to(qt_ref[:, colx:colx + 1], (HEAD, HEAD))
            r = jnp.sum(kc * s, axis=0, keepdims=True)
            u = beta * (v_ref[colx:colx + 1, :] - a * r)
            s = a * s + kc * u
            o_ref[colx:colx + 1, :] = jnp.sum(qc * s, axis=0, keepdims=True)
        s_ref[bl] = s


def _gdn_steps(act_tm, bg, s0, heads, seqs=8):
    steps, b, _ = act_tm.shape
    assert b % seqs == 0
    nbo = b // seqs
    width = heads * HEAD
    lanes = seqs * steps

    def cols(a):
        a = a.reshape(steps, nbo, seqs, heads, HEAD)
        return a.transpose(1, 3, 4, 2, 0).reshape(nbo, heads, HEAD, lanes)

    qt = cols(act_tm[..., :width])
    kt = cols(act_tm[..., width:2 * width])
    v = act_tm[..., 2 * width:].reshape(steps, nbo, seqs, heads, HEAD)
    v = v.transpose(1, 3, 2, 0, 4).reshape(nbo, heads, lanes, HEAD)
    beta = bg[:, :heads].reshape(-1)
    g = bg[:, heads:].reshape(-1)
    smem = pl.BlockSpec(memory_space=pltpu.SMEM)
    o, s_new = pl.pallas_call(
        functools.partial(_gdn_step_kernel, seqs=seqs, steps=steps, heads=heads),
        name="gdn_steps",
        out_shape=(jax.ShapeDtypeStruct((nbo, heads, lanes, HEAD), F32),
                   jax.ShapeDtypeStruct(s0.shape, F32)),
        grid=(nbo, heads),
        in_specs=[smem, smem,
                  pl.BlockSpec((None, None, HEAD, lanes), lambda i, h: (i, h, 0, 0)),
                  pl.BlockSpec((None, None, HEAD, lanes), lambda i, h: (i, h, 0, 0)),
                  pl.BlockSpec((None, None, lanes, HEAD), lambda i, h: (i, h, 0, 0)),
                  pl.BlockSpec((seqs, None, HEAD, HEAD), lambda i, h: (i, h, 0, 0))],
        out_specs=(pl.BlockSpec((None, None, lanes, HEAD), lambda i, h: (i, h, 0, 0)),
                   pl.BlockSpec((seqs, None, HEAD, HEAD), lambda i, h: (i, h, 0, 0))),
        compiler_params=_params("parallel", "parallel"),
    )(beta, g, qt, kt, v, s0)
    o = o.reshape(nbo, heads, seqs, steps, HEAD).transpose(0, 2, 3, 1, 4).reshape(b * steps, width)
    return o, s_new


def _gdn_out_gate(o, z, g):
    outs = []
    for h in range(o.shape[1] // HEAD):
        sl = slice(h * HEAD, (h + 1) * HEAD)
        outs.append(_rms(o[:, sl], g) * _silu(z[:, sl]))
    return jnp.concatenate(outs, axis=1)


def _mm_post_kernel(*refs, gated):
    if gated:
        a_ref, z_ref, ng_ref, w_ref, x_ref, gt_ref, pg_ref, o_ref = refs
        a = _gdn_out_gate(a_ref[...], z_ref[...], ng_ref[...]).astype(BF16)
    else:
        a_ref, w_ref, x_ref, gt_ref, pg_ref, o_ref = refs
        a = a_ref[...].astype(BF16)
    y = _dot(a, w_ref[...])
    o_ref[...] = x_ref[...] + gt_ref[...] * _rms(y, pg_ref[...])


def _matmul_post(a, w, x, gt, post_g, z=None, z_col=0, norm_g=None):
    m, k = a.shape
    d = w.shape[1]
    tm = _tile(m, 256)
    gated = z is not None
    in_specs = [pl.BlockSpec((tm, k), lambda i: (i, 0))]
    args = [a]
    if gated:
        in_specs += [pl.BlockSpec((tm, k), lambda i: (i, z_col)), pl.BlockSpec((1, HEAD), lambda i: (0, 0))]
        args += [z, norm_g.reshape(1, HEAD)]
    in_specs += [pl.BlockSpec((k, d), lambda i: (0, 0)),
                 pl.BlockSpec((tm, d), lambda i: (i, 0)),
                 _mod_spec(gt, tm, d),
                 pl.BlockSpec((1, d), lambda i: (0, 0))]
    args += [w, x, gt, post_g.reshape(1, d)]
    return pl.pallas_call(
        functools.partial(_mm_post_kernel, gated=gated),
        name="matmul_post",
        out_shape=jax.ShapeDtypeStruct((m, d), F32),
        grid=(m // tm,),
        in_specs=in_specs,
        out_specs=pl.BlockSpec((tm, d), lambda i: (i, 0)),
        compiler_params=_params("parallel"),
    )(*args)


def _ffn_kernel(x_ref, g_ref, sc_ref, sh_ref, wg_ref, wu_ref, wd_ref, gt_ref, pg_ref, o_ref, h_scr, acc):
    f = pl.program_id(1)

    @pl.when(f == 0)
    def _():
        h_scr[...] = _norm_mod(x_ref[...], g_ref[...], sc_ref[...], sh_ref[...]).astype(BF16)
        acc[...] = jnp.zeros_like(acc)

    h = h_scr[...]
    act = (_silu(_dot(h, wg_ref[...])) * _dot(h, wu_ref[...])).astype(BF16)
    acc[...] += _dot(act, wd_ref[...])

    @pl.when(f == pl.num_programs(1) - 1)
    def _():
        o_ref[...] = x_ref[...] + gt_ref[...] * _rms(acc[...], pg_ref[...])


def _dense_ffn(x, g, sc, sh, w_gu, w_down, gt, post_g, tm_pref=512, tf_pref=512):
    m, d = x.shape
    ff = w_down.shape[0]
    tm = _tile(m, tm_pref)
    tf = _tile(ff, tf_pref, LANES)
    nf = ff // tf
    return pl.pallas_call(
        _ffn_kernel,
        name="dense_ffn",
        out_shape=jax.ShapeDtypeStruct((m, d), F32),
        grid=(m // tm, nf),
        in_specs=[pl.BlockSpec((tm, d), lambda i, f: (i, 0)),
                  pl.BlockSpec((1, d), lambda i, f: (0, 0)),
                  _mod_spec(sc, tm, d), _mod_spec(sh, tm, d),
                  pl.BlockSpec((d, tf), lambda i, f: (0, f)),
                  pl.BlockSpec((d, tf), lambda i, f: (0, nf + f)),
                  pl.BlockSpec((tf, d), lambda i, f: (f, 0)),
                  _mod_spec(gt, tm, d),
                  pl.BlockSpec((1, d), lambda i, f: (0, 0))],
        out_specs=pl.BlockSpec((tm, d), lambda i, f: (i, 0)),
        scratch_shapes=[pltpu.VMEM((tm, d), BF16), pltpu.VMEM((tm, d), F32)],
        compiler_params=_params("parallel", "arbitrary"),
    )(x, g.reshape(1, d), sc, sh, w_gu, w_gu, w_down, gt, post_g.reshape(1, d))


def _router_kernel(x_ref, g_ref, sc_ref, sh_ref, wr_ref, h_ref, r_ref, *, experts):
    h = _norm_mod(x_ref[...], g_ref[...], sc_ref[...], sh_ref[...])
    h_ref[...] = h.astype(BF16)
    logits = _dot3(h, wr_ref[...])
    lane = lax.broadcasted_iota(jnp.int32, logits.shape, 1)
    lanef = lane.astype(F32)
    logits = jnp.where(lane < experts, logits, NEG_INF)
    m1 = jnp.max(logits, axis=-1, keepdims=True)
    i1 = jnp.min(jnp.where(logits == m1, lanef, float(LANES)), axis=-1, keepdims=True)
    rest = jnp.where(lanef == i1, NEG_INF, logits)
    m2 = jnp.max(rest, axis=-1, keepdims=True)
    i2 = jnp.min(jnp.where(rest == m2, lanef, float(LANES)), axis=-1, keepdims=True)
    e2 = jnp.exp(m2 - m1)
    w1 = 1.0 / (1.0 + e2)
    w2 = e2 / (1.0 + e2)
    r_ref[...] = jnp.where(lane == 0, i1, jnp.where(lane == 1, i2, jnp.where(lane == 2, w1, jnp.where(lane == 3, w2, 0.0))))


def _router(x, g, sc, sh, w_router):
    m, d = x.shape
    experts = w_router.shape[1]
    tm = _tile(m, 512)
    wr = jnp.pad(w_router, ((0, 0), (0, LANES - experts)))
    return pl.pallas_call(
        functools.partial(_router_kernel, experts=experts),
        name="moe_router",
        out_shape=(jax.ShapeDtypeStruct((m, d), BF16), jax.ShapeDtypeStruct((m, LANES), F32)),
        grid=(m // tm,),
        in_specs=[pl.BlockSpec((tm, d), lambda i: (i, 0)),
                  pl.BlockSpec((1, d), lambda i: (0, 0)),
                  _mod_spec(sc, tm, d), _mod_spec(sh, tm, d),
                  pl.BlockSpec((d, LANES), lambda i: (0, 0))],
        out_specs=(pl.BlockSpec((tm, d), lambda i: (i, 0)), pl.BlockSpec((tm, LANES), lambda i: (i, 0))),
        compiler_params=_params("parallel"),
    )(x, g.reshape(1, d), sc, sh, wr)


def _moe_ffn_kernel(te_ref, nu_ref, h_ref, wg_ref, wu_ref, wd_ref, o_ref, acc):
    i = pl.program_id(0)
    f = pl.program_id(1)
    last = pl.num_programs(1) - 1
    used = i < nu_ref[0]

    @pl.when(jnp.logical_and(used, f == 0))
    def _():
        acc[...] = jnp.zeros_like(acc)

    @pl.when(used)
    def _():
        h = h_ref[...]
        act = (_silu(_dot(h, wg_ref[...].astype(BF16))) * _dot(h, wu_ref[...].astype(BF16))).astype(BF16)
        acc[...] += _dot(act, wd_ref[...].astype(BF16))

    @pl.when(jnp.logical_and(used, f == last))
    def _():
        o_ref[...] = acc[...]

    @pl.when(jnp.logical_and(jnp.logical_not(used), f == last))
    def _():
        o_ref[...] = jnp.zeros_like(o_ref)


def _moe_ffn(hs, tile_expert, n_used, w_gu, w_down, tm, tf_pref=512):
    mp, d = hs.shape
    ff = w_down.shape[1]
    tf = _tile(ff, tf_pref, LANES)
    nf = ff // tf

    def fidx(i, f, nu):
        return jnp.where(i < nu[0], f, nf - 1)

    grid_spec = pltpu.PrefetchScalarGridSpec(
        num_scalar_prefetch=2,
        grid=(mp // tm, nf),
        in_specs=[pl.BlockSpec((tm, d), lambda i, f, te, nu: (i, 0), pipeline_mode=pl.Buffered(1)),
                  pl.BlockSpec((None, d, tf), lambda i, f, te, nu: (te[i], 0, fidx(i, f, nu))),
                  pl.BlockSpec((None, d, tf), lambda i, f, te, nu: (te[i], 0, nf + fidx(i, f, nu))),
                  pl.BlockSpec((None, tf, d), lambda i, f, te, nu: (te[i], fidx(i, f, nu), 0))],
        out_specs=pl.BlockSpec((tm, d), lambda i, f, te, nu: (i, 0), pipeline_mode=pl.Buffered(1)),
        scratch_shapes=[pltpu.VMEM((tm, d), F32)],
    )
    return pl.pallas_call(
        _moe_ffn_kernel,
        name="moe_ffn",
        out_shape=jax.ShapeDtypeStruct((mp, d), F32),
        grid_spec=grid_spec,
        compiler_params=_params("arbitrary", "arbitrary"),
    )(tile_expert, n_used, hs, w_gu, w_gu, w_down)


def _combine_kernel(y1_ref, y2_ref, r_ref, x_ref, gt_ref, pg_ref, o_ref):
    r = r_ref[...]
    y = r[:, 2:3] * y1_ref[...] + r[:, 3:4] * y2_ref[...]
    o_ref[...] = x_ref[...] + gt_ref[...] * _rms(y, pg_ref[...])


def _moe_combine(y1, y2, route, x, gt, post_g):
    m, d = x.shape
    tm = _tile(m, 256)
    return pl.pallas_call(
        _combine_kernel,
        name="moe_combine",
        out_shape=jax.ShapeDtypeStruct((m, d), F32),
        grid=(m // tm,),
        in_specs=[pl.BlockSpec((tm, d), lambda i: (i, 0)),
                  pl.BlockSpec((tm, d), lambda i: (i, 0)),
                  pl.BlockSpec((tm, LANES), lambda i: (i, 0)),
                  pl.BlockSpec((tm, d), lambda i: (i, 0)),
                  _mod_spec(gt, tm, d),
                  pl.BlockSpec((1, d), lambda i: (0, 0))],
        out_specs=pl.BlockSpec((tm, d), lambda i: (i, 0)),
        compiler_params=_params("parallel"),
    )(y1, y2, route, x, gt, post_g.reshape(1, d))


def _moe_layer(parts, g, post_g, w_router, w_gu, w_down, tm=768):
    experts = w_router.shape[1]
    routed = [_router(x, g, sc, sh, w_router) for x, sc, sh, _ in parts]
    h = jnp.concatenate([r[0] for r in routed], axis=0)
    route = jnp.concatenate([r[1] for r in routed], axis=0)
    m = h.shape[0]
    ids = route[:, :MOE_TOPK].astype(jnp.int32).reshape(-1)
    n_assign = m * MOE_TOPK
    hot = (ids[:, None] == jnp.arange(experts, dtype=jnp.int32)[None, :]).astype(jnp.int32)
    seen = jnp.cumsum(hot, axis=0)
    counts = seen[-1]
    padded = ((counts + tm - 1) // tm) * tm
    starts_p = jnp.cumsum(padded) - padded
    dest = jnp.sum(hot * (starts_p[None, :] + seen - 1), axis=1).astype(jnp.int32)
    n_tiles = -(-(n_assign + experts * (tm - 1)) // tm)
    mp = n_tiles * tm
    src_tok = jnp.zeros((mp,), jnp.int32).at[dest].set(jnp.arange(n_assign, dtype=jnp.int32) // MOE_TOPK)
    tile_start = jnp.arange(n_tiles, dtype=jnp.int32) * tm
    ends_p = jnp.cumsum(padded)
    tile_expert = jnp.sum((ends_p[None, :] <= tile_start[:, None]).astype(jnp.int32), axis=1)
    tile_expert = jnp.minimum(tile_expert, experts - 1)
    n_used = (ends_p[-1] // tm).astype(jnp.int32).reshape(1)
    last_e = tile_expert[jnp.maximum(n_used[0] - 1, 0)]
    tile_expert = jnp.where(jnp.arange(n_tiles) < n_used[0], tile_expert, last_e)
    hs = jnp.take(h, src_tok, axis=0, mode="clip")
    ys = _moe_ffn(hs, tile_expert, n_used, w_gu, w_down, tm)
    dest = dest.reshape(m, MOE_TOPK)
    outs = []
    row0 = 0
    for (x, _, _, gt), (_, route_x) in zip(parts, routed):
        rows = slice(row0, row0 + x.shape[0])
        row0 += x.shape[0]
        y1 = jnp.take(ys, dest[rows, 0], axis=0, mode="clip")
        y2 = jnp.take(ys, dest[rows, 1], axis=0, mode="clip")
        outs.append(_moe_combine(y1, y2, route_x, x, gt, post_g))
    return outs


def _block_mean_kernel(k_ref, o_ref):
    o_ref[0] = jnp.mean(k_ref[...], axis=0, keepdims=True)


def _block_means(k):
    t_len, width = k.shape
    nb = t_len // MOBA_BLOCK
    out = pl.pallas_call(
        _block_mean_kernel,
        name="moba_block_means",
        out_shape=jax.ShapeDtypeStruct((nb, 1, width), F32),
        grid=(nb,),
        in_specs=[pl.BlockSpec((MOBA_BLOCK, width), lambda j: (j, 0))],
        out_specs=pl.BlockSpec((1, 1, width), lambda j: (j, 0, 0)),
        compiler_params=_params("parallel"),
    )(k)
    return out.reshape(nb, width)


def _moba_seq_kernel(q_ref, k_ref, vt_ref, km_ref, o_ref, *, nb, hb):
    bs = MOBA_BLOCK
    ob = pl.program_id(1)
    own = pl.multiple_of(ob * bs, bs)
    causal = lax.broadcasted_iota(jnp.int32, (bs, bs), 0) <= lax.broadcasted_iota(jnp.int32, (bs, bs), 1)
    blk = lax.broadcasted_iota(jnp.int32, (bs, LANES), 1)
    blk_t = lax.broadcasted_iota(jnp.int32, (LANES, bs), 0)
    blkf_t = blk_t.astype(F32)
    hsl = [slice(h * HEAD, (h + 1) * HEAD) for h in range(hb)]

    qts = []
    for h in range(hb):
        q = q_ref[:, hsl[h]]
        gate = _dot3(km_ref[:, hsl[h]], q, _dot_nt)
        gate = jnp.where(blk_t < ob, gate, NEG_INF)
        sel = jnp.zeros((LANES, bs), F32)
        for _ in range(MOBA_TOPK):
            m = jnp.max(gate, axis=0, keepdims=True)
            first = jnp.min(jnp.where(gate == m, blkf_t, float(LANES)), axis=0, keepdims=True)
            pick = jnp.logical_and(blkf_t == first, m > NEG_INF)
            sel = jnp.where(pick, 1.0, sel)
            gate = jnp.where(pick, NEG_INF, gate)
        bias_t = jnp.where(sel > 0.0, 0.0, MASKED)
        q_t = (q * (HEAD ** -0.5 * math.log2(math.e))).T
        qts.append(jnp.concatenate([q_t, bias_t], axis=0).astype(BF16))

    def scores(h, start, hot):
        k_ext = jnp.concatenate([k_ref[pl.ds(start, bs), hsl[h]], hot], axis=1)
        return _dot(k_ext, qts[h])

    def one_hot(j):
        return jnp.where(blk == j, 1.0, 0.0).astype(BF16)

    no_hot = jnp.zeros((bs, LANES), BF16)
    ms, ls, accs = [], [], []
    for h in range(hb):
        s = jnp.where(causal, scores(h, own, no_hot), MASKED)
        m = jnp.max(s, axis=0, keepdims=True)
        p = jnp.exp2(s - m)
        ms.append(m)
        ls.append(jnp.sum(p, axis=0, keepdims=True))
        accs.append(_dot(vt_ref[ob, hsl[h], :], p.astype(BF16)))

    def body(j, carry):
        ss, ms, ls, accs = carry
        jn = jnp.minimum(j + 1, nb - 1)
        s_next = tuple(scores(h, pl.multiple_of(jn * bs, bs), one_hot(jn)) for h in range(hb))
        m_out, l_out, a_out = [], [], []
        for h in range(hb):
            m_new = jnp.maximum(ms[h], jnp.max(ss[h], axis=0, keepdims=True))
            alpha = jnp.exp2(ms[h] - m_new)
            p = jnp.exp2(ss[h] - m_new)
            m_out.append(m_new)
            l_out.append(alpha * ls[h] + jnp.sum(p, axis=0, keepdims=True))
            a_out.append(alpha * accs[h] + _dot(vt_ref[j, hsl[h], :], p.astype(BF16)))
        return s_next, tuple(m_out), tuple(l_out), tuple(a_out)

    s_first = tuple(scores(h, 0, one_hot(0)) for h in range(hb))
    _, _, ls, accs = lax.fori_loop(0, ob, body, (s_first, tuple(ms), tuple(ls), tuple(accs)))
    for h in range(hb):
        o_ref[:, hsl[h]] = (accs[h] / ls[h]).T.astype(o_ref.dtype)


def _moba_seq(q, k, kb, vt, heads, hb=4):
    t_len, width = q.shape
    bs = MOBA_BLOCK
    assert t_len % bs == 0 and heads % hb == 0
    nb = t_len // bs
    assert nb <= LANES
    kmean = jnp.pad(_block_means(k), ((0, LANES - nb), (0, 0)))
    return pl.pallas_call(
        functools.partial(_moba_seq_kernel, nb=nb, hb=hb),
        name="moba_seq",
        out_shape=jax.ShapeDtypeStruct((t_len, width), BF16),
        grid=(heads // hb, nb),
        in_specs=[pl.BlockSpec((bs, hb * HEAD), lambda h, i: (i, h)),
                  pl.BlockSpec((t_len, hb * HEAD), lambda h, i: (0, h)),
                  pl.BlockSpec((nb, hb * HEAD, bs), lambda h, i: (0, h, 0)),
                  pl.BlockSpec((LANES, hb * HEAD), lambda h, i: (0, h))],
        out_specs=pl.BlockSpec((bs, hb * HEAD), lambda h, i: (i, h)),
        compiler_params=_params("parallel", "arbitrary"),
    )(q, kb, vt, kmean)


def _head_match_bias(n_rows, heads, cols_used):
    row = lax.broadcasted_iota(jnp.int32, (n_rows, LANES), 0)
    col = lax.broadcasted_iota(jnp.int32, (n_rows, LANES), 1)
    keep = jnp.logical_and(row % heads == col % heads, col < cols_used)
    return keep, row, col


def _moba_paged_kernel(pt_ref, qt_ref, kn_ref, vn_ref, *refs, heads, steps, nblk, page, bpb):
    k_refs = refs[:2 * bpb]
    v_refs = refs[2 * bpb:4 * bpb]
    o_ref, qtb, bias, acc, ks, st_m, st_l, st_w = refs[4 * bpb:]
    j = pl.program_id(1)
    cols = steps * heads
    n_rows = page * heads

    @pl.when(j == 0)
    def _():
        qtb[...] = (qt_ref[0] * (HEAD ** -0.5 * math.log2(math.e))).astype(BF16)
        keep, _, _ = _head_match_bias(n_rows, heads, cols)
        bias[...] = jnp.where(keep, 0.0, MASKED)

    pages = range(2 * bpb)
    s = [_dot(k_refs[p][0].astype(BF16), qtb[...]) + bias[...] for p in pages]
    pmax = [jnp.max(s[p], axis=0, keepdims=True) for p in pages]
    m = [jnp.maximum(pmax[2 * u], pmax[2 * u + 1]) for u in range(bpb)]
    prob = [jnp.exp2(s[p] - m[p // 2]) for p in pages]
    psum = [jnp.sum(prob[p], axis=0, keepdims=True) for p in pages]
    pv = [_dot(prob[p].T.astype(BF16), v_refs[p][0].astype(BF16)) for p in pages]
    ksum = [jnp.sum(k_refs[p][0].reshape(page, heads, HEAD), axis=0) for p in pages]
    for u in range(bpb):
        blk = j * bpb + u
        acc[blk] = pv[2 * u] + pv[2 * u + 1]
        ks[blk] = ksum[2 * u] + ksum[2 * u + 1]
        st_m[pl.ds(blk, 1), :] = m[u]
        st_l[pl.ds(blk, 1), :] = psum[2 * u] + psum[2 * u + 1]

    @pl.when(j == nblk // bpb - 1)
    def _():
        qt = qt_ref[0]
        hrow = lax.broadcasted_iota(jnp.int32, (heads, LANES), 0)
        hcol = lax.broadcasted_iota(jnp.int32, (heads, LANES), 1) % heads
        gates = []
        for b in range(nblk):
            g_all = _dot3(ks[b], qt) * (1.0 / (2 * page))
            gates.append(jnp.sum(jnp.where(hrow == hcol, g_all, 0.0), axis=0, keepdims=True))
        sel = []
        for b in range(nblk):
            rank = jnp.zeros((1, LANES), F32)
            for o in range(nblk):
                if o == b:
                    continue
                ahead = gates[o] > gates[b]
                if o < b:
                    ahead = jnp.logical_or(ahead, gates[o] == gates[b])
                rank = rank + jnp.where(ahead, 1.0, 0.0)
            sel.append(rank < float(MOBA_TOPK))
        keep, row, col = _head_match_bias(cols, heads, cols)
        keep = jnp.logical_and(keep, row // heads <= col // heads)
        s_own = _dot(kn_ref[0].astype(BF16), qtb[...]) + jnp.where(keep, 0.0, MASKED)
        m_own = jnp.max(s_own, axis=0, keepdims=True)
        p_own = jnp.exp2(s_own - m_own)
        l_own = jnp.sum(p_own, axis=0, keepdims=True)
        v_own = vn_ref[0]
        if cols < LANES:
            p_own = jnp.concatenate([p_own, jnp.zeros((LANES - cols, LANES), F32)], axis=0)
            v_own = jnp.concatenate([v_own, jnp.zeros((LANES - cols, HEAD), F32)], axis=0)
        acc_own = _dot(p_own.T.astype(BF16), v_own.astype(BF16))
        m_all = m_own
        for b in range(nblk):
            m_all = jnp.maximum(m_all, jnp.where(sel[b], st_m[b:b + 1, :], MASKED))
        w_own = jnp.exp2(m_own - m_all)
        den = w_own * l_own
        ws = []
        for b in range(nblk):
            wb = jnp.where(sel[b], jnp.exp2(st_m[b:b + 1, :] - m_all), 0.0)
            den = den + wb * st_l[b:b + 1, :]
            ws.append(wb)
        inv = 1.0 / den
        st_w[...] = jnp.zeros_like(st_w)
        for b in range(nblk):
            st_w[b:b + 1, :] = ws[b] * inv
        st_w[nblk:nblk + 1, :] = w_own * inv
        eye_r = lax.broadcasted_iota(jnp.int32, (LANES, LANES), 0)
        eye_c = lax.broadcasted_iota(jnp.int32, (LANES, LANES), 1)
        eye = (eye_r == eye_c).astype(BF16)
        wr = st_w[...]
        w_hi, w_mid = _split(wr)
        w_lo = (wr - w_hi.astype(F32) - w_mid.astype(F32)).astype(BF16)
        wcol = _dot_nt(eye, w_hi) + (_dot_nt(eye, w_mid) + _dot_nt(eye, w_lo))
        out = wcol[:, nblk:nblk + 1] * acc_own
        for b in range(nblk):
            out = out + wcol[:, b:b + 1] * acc[b]
        o_ref[0] = out[:cols, :]


def _moba_paged(q, k_new, v_new, cache_k, cache_v, page_table, heads):
    b, steps, width = q.shape
    n_phys, page = cache_k.shape[0], cache_k.shape[1]
    n_pages = page_table.shape[1]
    assert MOBA_BLOCK == 2 * page and n_pages % 2 == 0 and steps <= MOBA_BLOCK
    nblk = n_pages // 2
    cols = steps * heads
    n_rows = page * heads
    stat_rows = -(-(nblk + 1) // 8) * 8
    assert cols <= LANES
    ck = cache_k.reshape(n_phys, n_rows, HEAD)
    cv = cache_v.reshape(n_phys, n_rows, HEAD)
    pt = page_table.reshape(-1).astype(jnp.int32)
    qt = jnp.pad(q.reshape(b, cols, HEAD).transpose(0, 2, 1), ((0, 0), (0, 0), (0, LANES - cols)))
    kn = k_new.reshape(b, cols, HEAD)
    vn = v_new.reshape(b, cols, HEAD)

    bpb = 2 if nblk % 2 == 0 else 1
    ppb = 2 * bpb

    def page_spec(off):
        return pl.BlockSpec((1, n_rows, HEAD), lambda i, j, pt: (pt[i * n_pages + ppb * j + off], 0, 0))

    page_specs = [page_spec(off) for off in range(ppb)]
    new_spec = pl.BlockSpec((1, cols, HEAD), lambda i, j, pt: (i, 0, 0))
    grid_spec = pltpu.PrefetchScalarGridSpec(
        num_scalar_prefetch=1,
        grid=(b, nblk // bpb),
        in_specs=[pl.BlockSpec((1, HEAD, LANES), lambda i, j, pt: (i, 0, 0)), new_spec, new_spec]
        + page_specs + page_specs,
        out_specs=pl.BlockSpec((1, cols, HEAD), lambda i, j, pt: (i, 0, 0)),
        scratch_shapes=[pltpu.VMEM((HEAD, LANES), BF16), pltpu.VMEM((n_rows, LANES), F32),
                        pltpu.VMEM((nblk, LANES, HEAD), F32), pltpu.VMEM((nblk, heads, HEAD), F32),
                        pltpu.VMEM((stat_rows, LANES), F32), pltpu.VMEM((stat_rows, LANES), F32),
                        pltpu.VMEM((stat_rows, LANES), F32)],
    )
    out = pl.pallas_call(
        functools.partial(_moba_paged_kernel, heads=heads, steps=steps, nblk=nblk, page=page, bpb=bpb),
        name="moba_paged",
        out_shape=jax.ShapeDtypeStruct((b, cols, HEAD), F32),
        grid_spec=grid_spec,
        compiler_params=_params("parallel", "arbitrary"),
    )(pt, qt, kn, vn, *([ck] * ppb), *([cv] * ppb))
    return out.reshape(b, steps, width)


def _rope_tables(pos):
    half = HEAD // 2
    inv = ROPE_THETA ** (-jnp.arange(half, dtype=F32) / half)
    ang = pos.astype(F32)[:, None] * inv[None, :]
    cos = jnp.cos(ang)
    sin = jnp.sin(ang)
    return jnp.concatenate([cos, cos], axis=1), jnp.concatenate([-sin, sin], axis=1)


def _chunks(mod, rep):
    parts = jnp.split(mod, 6, axis=-1)
    return [jnp.repeat(p, rep, axis=0) if rep > 1 else p for p in parts]


def kernel(x_prompt, x_sample, cache_k, cache_v, page_table, state_delta, state_conv, c_prompt, c_sample, ada_w, ada_b, ln_mix_pre, ln_mix_post, ln_ffn_pre, ln_ffn_post, a_w_in, a_conv_w, a_A_log, a_dt_bias, a_norm_g, a_w_out, kv_ada_w, kv_ada_b, kv_norm_g, w_kv, b_w_q, b_w_out, f_w_gu, f_w_down, m_router, m_w_gu, m_w_down):
    bp, t_p, d = x_prompt.shape
    bs, t_s, _ = x_sample.shape
    assert bp == 1
    gh = a_A_log.shape[1]
    ah = cache_k.shape[2]
    gw = gh * HEAD
    aw = ah * HEAD
    past = page_table.shape[1] * cache_k.shape[1]

    c_all = jnp.concatenate([c_prompt, c_sample], axis=0)
    r_all = c_all.shape[0]
    c_all = jnp.pad(c_all, ((0, (-r_all) % 8), (0, 0)))
    mods = [_modulation(c_all, ada_w, ada_b, l) for l in range(2)]
    kv_mod = _modulation(c_all, kv_ada_w[None], kv_ada_b[None])

    def mod_p(m):
        return m[0:1]

    def mod_s(m):
        return m[1:1 + bs]

    xp = x_prompt.reshape(t_p, d)
    xs = x_sample.reshape(bs * t_s, d)

    w_in = a_w_in[0]
    n_conv = 3 * gw
    w_ba = w_in[:, n_conv + gw:]
    w_out0 = a_w_out[0].astype(BF16)

    sh_m, sc_m, gt_m, sh_f, sc_f, gt_f = _chunks(mod_p(mods[0]), 1)
    proj_p = _norm_mod_matmul(xp, ln_mix_pre[0], sc_m, sh_m, w_in, n=n_conv + gw)
    bg_p = _gdn_gates(xp, ln_mix_pre[0], sc_m, sh_m, w_ba, a_A_log[0], a_dt_bias[0])
    act_p = _gdn_conv_seq(proj_p, a_conv_w[0], jnp.zeros((CONV_W - 1, n_conv), F32), gw)
    o_p, delta_p = _gdn_chunked(act_p, bg_p, jnp.zeros((gh, HEAD, HEAD), F32), gh)
    conv_p = proj_p[t_p - (CONV_W - 1):, :n_conv]
    xp = _matmul_post(o_p, w_out0, xp, gt_m, ln_mix_post[0], z=proj_p, z_col=3, norm_g=a_norm_g[0])

    ssh_m, ssc_m, sgt_m, ssh_f, ssc_f, sgt_f = _chunks(mod_s(mods[0]), t_s)
    proj_s = _norm_mod_matmul(xs, ln_mix_pre[0], ssc_m, ssh_m, w_in, n=n_conv + gw)
    bg_s = _gdn_gates(xs, ln_mix_pre[0], ssc_m, ssh_m, w_ba, a_A_log[0], a_dt_bias[0])
    qkv_s = proj_s[:, :n_conv].reshape(bs, t_s, n_conv)
    xc_s = jnp.concatenate([state_conv[:, 0].transpose(1, 0, 2), qkv_s.transpose(1, 0, 2)], axis=0)
    act_s = _gdn_conv_steps(xc_s, a_conv_w[0], gw)
    o_s, delta_s = _gdn_steps(act_s, bg_s, state_delta[:, 0], gh)
    conv_s = qkv_s[:, t_s - (CONV_W - 1):]
    xs = _matmul_post(o_s, w_out0, xs, sgt_m, ln_mix_post[0], z=proj_s, z_col=3, norm_g=a_norm_g[0])

    w_gu0 = f_w_gu[0].astype(BF16)
    w_dn0 = f_w_down[0].astype(BF16)
    xp = _dense_ffn(xp, ln_ffn_pre[0], sc_f, sh_f, w_gu0, w_dn0, gt_f, ln_ffn_post[0])
    xs = _dense_ffn(xs, ln_ffn_pre[0], ssc_f, ssh_f, w_gu0, w_dn0, sgt_f, ln_ffn_post[0])

    cos_p, sin_p = _rope_tables(jnp.arange(t_p))
    cos_s, sin_s = _rope_tables(jnp.tile(past + jnp.arange(t_s), bs))
    ksh, ksc = jnp.split(kv_mod, 2, axis=-1)
    ksc_s = jnp.repeat(mod_s(ksc), t_s, axis=0)
    ksh_s = jnp.repeat(mod_s(ksh), t_s, axis=0)
    k_p, kb_p = _norm_mod_matmul(xp, kv_norm_g, mod_p(ksc), mod_p(ksh), w_kv, 0, aw, cos_p, sin_p, rope=True,
                                 copy="bf16")
    v_p, vt_p = _norm_mod_matmul(xp, kv_norm_g, mod_p(ksc), mod_p(ksh), w_kv, aw, aw, copy="bf16_blocks_t")
    k_s = _norm_mod_matmul(xs, kv_norm_g, ksc_s, ksh_s, w_kv, 0, aw, cos_s, sin_s, rope=True)
    v_s = _norm_mod_matmul(xs, kv_norm_g, ksc_s, ksh_s, w_kv, aw, aw)

    w_out1 = b_w_out[0].astype(BF16)
    sh_m, sc_m, gt_m, sh_f, sc_f, gt_f = _chunks(mod_p(mods[1]), 1)
    ssh_m, ssc_m, sgt_m, ssh_f, ssc_f, sgt_f = _chunks(mod_s(mods[1]), t_s)
    q_p = _norm_mod_matmul(xp, ln_mix_pre[1], sc_m, sh_m, b_w_q[0], cos=cos_p, sin=sin_p, rope=True)
    q_s = _norm_mod_matmul(xs, ln_mix_pre[1], ssc_m, ssh_m, b_w_q[0], cos=cos_s, sin=sin_s, rope=True)
    att_p = _moba_seq(q_p, k_p, kb_p, vt_p, ah)
    att_s = _moba_paged(q_s.reshape(bs, t_s, aw), k_s.reshape(bs, t_s, aw), v_s.reshape(bs, t_s, aw),
                        cache_k, cache_v, page_table, ah)
    xp = _matmul_post(att_p, w_out1, xp, gt_m, ln_mix_post[1])
    xs = _matmul_post(att_s.reshape(bs * t_s, aw), w_out1, xs, sgt_m, ln_mix_post[1])

    y_p, y_s = _moe_layer([(xp, sc_f, sh_f, gt_f), (xs, ssc_f, ssh_f, sgt_f)], ln_ffn_pre[1], ln_ffn_post[1],
                          m_router[0], m_w_gu[0], m_w_down[0])

    return (y_p.reshape(1, t_p, d), y_s.reshape(bs, t_s, d),
            delta_p.reshape(1, 1, gh, HEAD, HEAD), conv_p.reshape(1, 1, CONV_W - 1, n_conv),
            k_p.reshape(1, t_p, ah, HEAD), v_p.reshape(1, t_p, ah, HEAD),
            delta_s.reshape(bs, 1, gh, HEAD, HEAD), conv_s.reshape(bs, 1, CONV_W - 1, n_conv),
            k_s.reshape(bs, t_s, ah, HEAD), v_s.reshape(bs, t_s, ah, HEAD))
```

```python
import functools
import math

import jax
import jax.numpy as jnp
from jax import lax
from jax.experimental import pallas as pl
from jax.experimental.pallas import tpu as pltpu

F32 = jnp.float32
BF16 = jnp.bfloat16
NEG_INF = float("-inf")
MASKED = -1e30

EPS = 1e-6
HEAD = 128
GDN_CHUNK = 64
CONV_W = 4
MOBA_BLOCK = 256
MOBA_TOPK = 3
ROPE_THETA = 10000.0
MOE_TOPK = 2
LANES = 128
VMEM_LIMIT = 52 * 1024 * 1024


def _params(*sem):
    return pltpu.CompilerParams(dimension_semantics=sem, vmem_limit_bytes=VMEM_LIMIT)


def _tile(n, pref, mult=8):
    if n <= pref:
        return n
    t = (pref // mult) * mult
    while t >= mult:
        if n % t == 0:
            return t
        t -= mult
    return n


def _dot(a, b):
    return jnp.dot(a, b, preferred_element_type=F32)


def _dot_nt(a, b):
    return lax.dot_general(a, b, (((1,), (1,)), ((), ())), preferred_element_type=F32)


def _dot_tn(a, b):
    return lax.dot_general(a, b, (((0,), (0,)), ((), ())), preferred_element_type=F32)


def _split(a):
    hi = a.astype(BF16)
    lo = (a - hi.astype(F32)).astype(BF16)
    return hi, lo


def _dot3(a, b, dot=_dot):
    ah, al = _split(a)
    bh, bl = _split(b)
    return dot(ah, bh) + (dot(ah, bl) + dot(al, bh))


def _dot_ones(ones, a, left):
    hi, mid = _split(a)
    lo = (a - hi.astype(F32) - mid.astype(F32)).astype(BF16)
    if left:
        return _dot(ones, hi) + (_dot(ones, mid) + _dot(ones, lo))
    return _dot(hi, ones) + (_dot(mid, ones) + _dot(lo, ones))


def _sigmoid(x):
    return 1.0 / (1.0 + jnp.exp(-x))


def _silu(x):
    return x * _sigmoid(x)


def _softplus(x):
    return jnp.maximum(x, 0.0) + jnp.log(1.0 + jnp.exp(-jnp.abs(x)))


def _rms(x, g):
    return x * lax.rsqrt(jnp.mean(x * x, axis=-1, keepdims=True) + EPS) * g


def _norm_mod(x, g, sc, sh):
    return _rms(x, g) * (1.0 + sc) + sh


def _mod_spec(mod, tm, k):
    if mod.shape[0] == 1:
        return pl.BlockSpec((1, k), lambda i, *_: (0, 0))
    return pl.BlockSpec((tm, k), lambda i, *_: (i, 0))


def _mod_kernel(c_ref, w_ref, b_ref, o_ref):
    o_ref[...] = _dot3(_silu(c_ref[...]), w_ref[...]) + b_ref[...]


def _modulation(c, w, b, layer=0):
    r, d = c.shape
    n = w.shape[2]
    tn = _tile(n, 512, LANES)
    return pl.pallas_call(
        _mod_kernel,
        name="adaln_modulation",
        out_shape=jax.ShapeDtypeStruct((r, n), F32),
        grid=(n // tn,),
        in_specs=[pl.BlockSpec((r, d), lambda j: (0, 0)),
                  pl.BlockSpec((None, d, tn), lambda j: (layer, 0, j)),
                  pl.BlockSpec((None, 1, tn), lambda j: (layer, 0, j))],
        out_specs=pl.BlockSpec((r, tn), lambda j: (0, j)),
        compiler_params=_params("parallel"),
    )(c, w, b.reshape(b.shape[0], 1, n))


def _rope_tile(y, cos, sin):
    outs = []
    for h in range(y.shape[1] // HEAD):
        yh = y[:, h * HEAD:(h + 1) * HEAD]
        outs.append(yh * cos + pltpu.roll(yh, HEAD // 2, 1) * sin)
    return outs[0] if len(outs) == 1 else jnp.concatenate(outs, axis=1)


def _nm_mm_kernel(x_ref, g_ref, sc_ref, sh_ref, w_ref, cos_ref, sin_ref, o_ref, *rest, rope, copy):
    j = pl.program_id(1)
    h_scr = rest[-1]

    @pl.when(j == 0)
    def _():
        h_scr[...] = _norm_mod(x_ref[...], g_ref[...], sc_ref[...], sh_ref[...]).astype(BF16)

    y = _dot(h_scr[...], w_ref[...].astype(BF16))
    if rope:
        y = _rope_tile(y, cos_ref[...], sin_ref[...])
    o_ref[...] = y
    if copy == "bf16":
        rest[0][...] = y.astype(BF16)
    elif copy == "bf16_blocks_t":
        for c in range(y.shape[0] // MOBA_BLOCK):
            rest[0][c] = y[c * MOBA_BLOCK:(c + 1) * MOBA_BLOCK, :].T.astype(BF16)


def _norm_mod_matmul(x, g, sc, sh, w, col0=0, n=None, cos=None, sin=None, rope=False, copy=None,
                     tm_pref=1024, tn_pref=512):
    m, k = x.shape
    n = w.shape[1] - col0 if n is None else n
    tm = _tile(m, tm_pref)
    tn = _tile(math.gcd(n, col0) if col0 else n, tn_pref, LANES)
    if cos is None:
        cos = jnp.zeros((m, HEAD), F32)
        sin = cos
    assert col0 % tn == 0 and n % tn == 0
    c0 = col0 // tn
    out_shape = jax.ShapeDtypeStruct((m, n), F32)
    out_specs = pl.BlockSpec((tm, tn), lambda i, j: (i, j))
    if copy == "bf16":
        out_shape = (out_shape, jax.ShapeDtypeStruct((m, n), BF16))
        out_specs = (out_specs, pl.BlockSpec((tm, tn), lambda i, j: (i, j)))
    elif copy == "bf16_blocks_t":
        assert tm % MOBA_BLOCK == 0
        out_shape = (out_shape, jax.ShapeDtypeStruct((m // MOBA_BLOCK, n, MOBA_BLOCK), BF16))
        out_specs = (out_specs, pl.BlockSpec((tm // MOBA_BLOCK, tn, MOBA_BLOCK), lambda i, j: (i, j, 0)))
    return pl.pallas_call(
        functools.partial(_nm_mm_kernel, rope=rope, copy=copy),
        name="norm_mod_matmul",
        out_shape=out_shape,
        grid=(m // tm, n // tn),
        in_specs=[pl.BlockSpec((tm, k), lambda i, j: (i, 0)),
                  pl.BlockSpec((1, k), lambda i, j: (0, 0)),
                  _mod_spec(sc, tm, k), _mod_spec(sh, tm, k),
                  pl.BlockSpec((k, tn), lambda i, j: (0, c0 + j)),
                  pl.BlockSpec((tm, HEAD), lambda i, j: (i, 0)),
                  pl.BlockSpec((tm, HEAD), lambda i, j: (i, 0))],
        out_specs=out_specs,
        scratch_shapes=[pltpu.VMEM((tm, k), BF16)],
        compiler_params=_params("parallel", "arbitrary"),
    )(x, g.reshape(1, k), sc, sh, w, cos, sin)


def _gates_kernel(x_ref, g_ref, sc_ref, sh_ref, w_ref, alog_ref, dtb_ref, o_ref, *, heads):
    h = _norm_mod(x_ref[...], g_ref[...], sc_ref[...], sh_ref[...])
    ba = _dot3(h, w_ref[...])
    lane = lax.broadcasted_iota(jnp.int32, ba.shape, 1)
    decay = -jnp.exp(alog_ref[...]) * _softplus(ba + dtb_ref[...])
    o_ref[...] = jnp.where(lane < heads, _sigmoid(ba), decay)


def _gdn_gates(x, g, sc, sh, w_ba, a_log, dt_bias):
    m, k = x.shape
    heads = a_log.shape[0]
    tm = _tile(m, 512)
    pad = jnp.zeros((heads,), F32)
    alog2 = jnp.concatenate([pad, a_log]).reshape(1, 2 * heads)
    dtb2 = jnp.concatenate([pad, dt_bias]).reshape(1, 2 * heads)
    return pl.pallas_call(
        functools.partial(_gates_kernel, heads=heads),
        name="gdn_gates",
        out_shape=jax.ShapeDtypeStruct((m, 2 * heads), F32),
        grid=(m // tm,),
        in_specs=[pl.BlockSpec((tm, k), lambda i: (i, 0)),
                  pl.BlockSpec((1, k), lambda i: (0, 0)),
                  _mod_spec(sc, tm, k), _mod_spec(sh, tm, k),
                  pl.BlockSpec((k, 2 * heads), lambda i: (0, 0)),
                  pl.BlockSpec((1, 2 * heads), lambda i: (0, 0)),
                  pl.BlockSpec((1, 2 * heads), lambda i: (0, 0))],
        out_specs=pl.BlockSpec((tm, 2 * heads), lambda i: (i, 0)),
        compiler_params=_params("parallel"),
    )(x, g.reshape(1, k), sc, sh, w_ba, alog2, dtb2)


def _qkv_act(acc, part):
    act = _silu(acc)
    outs = []
    for h in range(act.shape[1] // HEAD):
        a = act[:, h * HEAD:(h + 1) * HEAD]
        inv = lax.rsqrt(jnp.sum(a * a, axis=-1, keepdims=True) + EPS)
        fac = jnp.where(part == 0, inv * (HEAD ** -0.5), jnp.where(part == 1, inv, 1.0))
        outs.append(a * fac)
    return jnp.concatenate(outs, axis=1)


def _conv_seq_kernel(x_ref, w_ref, b0_ref, o_ref, xs, *, tt):
    part = pl.program_id(0)
    t = pl.program_id(1)

    @pl.when(t == 0)
    def _():
        xs[0:8, :] = b0_ref[...]

    @pl.when(t > 0)
    def _():
        xs[0:8, :] = xs[tt:tt + 8, :]

    xs[8:8 + tt, :] = x_ref[...]
    w = w_ref[...]
    acc = xs[5:5 + tt, :] * w[0:1, :]
    for j in range(1, CONV_W):
        acc = acc + xs[5 + j:5 + j + tt, :] * w[j:j + 1, :]
    o_ref[...] = _qkv_act(acc, part)


def _gdn_conv_seq(qkvz, conv_w, buf, width):
    t_len = qkvz.shape[0]
    tt = _tile(t_len, 512)
    b0 = jnp.concatenate([jnp.zeros((8 - (CONV_W - 1), 3 * width), F32), buf], axis=0)
    return pl.pallas_call(
        functools.partial(_conv_seq_kernel, tt=tt),
        name="gdn_conv_seq",
        out_shape=jax.ShapeDtypeStruct((t_len, 3 * width), F32),
        grid=(3, t_len // tt),
        in_specs=[pl.BlockSpec((tt, width), lambda c, t: (t, c)),
                  pl.BlockSpec((CONV_W, width), lambda c, t: (0, c)),
                  pl.BlockSpec((8, width), lambda c, t: (0, c))],
        out_specs=pl.BlockSpec((tt, width), lambda c, t: (t, c)),
        scratch_shapes=[pltpu.VMEM((tt + 8, width), F32)],
        compiler_params=_params("parallel", "arbitrary"),
    )(qkvz, conv_w, b0)


def _conv_step_kernel(x_ref, w_ref, o_ref, *, steps):
    part = pl.program_id(0)
    w = w_ref[...]
    for t in range(steps):
        acc = x_ref[t] * w[0:1, :]
        for j in range(1, CONV_W):
            acc = acc + x_ref[t + j] * w[j:j + 1, :]
        o_ref[t] = _qkv_act(acc, part)


def _gdn_conv_steps(xc, conv_w, width):
    rows, b, _ = xc.shape
    steps = rows - (CONV_W - 1)
    return pl.pallas_call(
        functools.partial(_conv_step_kernel, steps=steps),
        name="gdn_conv_steps",
        out_shape=jax.ShapeDtypeStruct((steps, b, 3 * width), F32),
        grid=(3,),
        in_specs=[pl.BlockSpec((rows, b, width), lambda c: (0, 0, c)),
                  pl.BlockSpec((CONV_W, width), lambda c: (0, c))],
        out_specs=pl.BlockSpec((steps, b, width), lambda c: (0, 0, c)),
        compiler_params=_params("parallel"),
    )(xc, conv_w)


def _unit_lower_inverses(mats, c):
    row = lax.broadcasted_iota(jnp.int32, (c, c), 0)
    col = lax.broadcasted_iota(jnp.int32, (c, c), 1)
    eye = (row == col).astype(F32)
    base = min(c, 16)
    same = (row // base) == (col // base)
    xs = [jnp.where(same, a, 0.0) for a in mats]
    ts = [eye - x for x in xs]
    for _ in range(int(math.log2(base)) - 1):
        xs = [_dot3(x, x) for x in xs]
        ts = [t + _dot3(t, x) for t, x in zip(ts, xs)]
    size = base
    while size < c:
        inner = same
        size *= 2
        same = (row // size) == (col // size)
        pick = jnp.logical_and(same, jnp.logical_not(inner))
        ys = [_dot3(jnp.where(pick, a, 0.0), t) for a, t in zip(mats, ts)]
        ts = [t - _dot3(t, y) for t, y in zip(ts, ys)]
    return ts


def _gdn_chunk_kernel(q_ref, k_ref, v_ref, bg_ref, gt_ref, s0_ref, o_ref, s_out_ref, s_scr, *, heads, c):
    n = pl.program_id(0)

    @pl.when(n == 0)
    def _():
        s_scr[...] = s0_ref[...]

    row = lax.broadcasted_iota(jnp.int32, (c, c), 0)
    col = lax.broadcasted_iota(jnp.int32, (c, c), 1)
    lower = row >= col
    strict = row > col
    bg = bg_ref[...]
    gcum = _dot_ones(lower.astype(BF16), bg, left=True)
    gcum_t = _dot_ones((row <= col).astype(BF16), gt_ref[0], left=False)

    hs = range(heads)
    sls = [slice(h * HEAD, (h + 1) * HEAD) for h in hs]
    q = [q_ref[:, sl] for sl in sls]
    k = [k_ref[:, sl] for sl in sls]
    gi = [gcum[:, heads + h:heads + h + 1] for h in hs]
    decay = [jnp.exp(jnp.where(lower, gi[h] - gcum_t[h:h + 1, :], NEG_INF)) for h in hs]
    kb = [k[h] * bg[:, h:h + 1] for h in hs]
    egi = [jnp.exp(g) for g in gi]
    qk_kk = [_dot3(jnp.concatenate([q[h], kb[h]], axis=0), k[h], _dot_nt) for h in hs]
    qk = [qk_kk[h][:c] * decay[h] for h in hs]
    ts = _unit_lower_inverses([jnp.where(strict, qk_kk[h][c:] * decay[h], 0.0) for h in hs], c)
    sol = [_dot3(ts[h], jnp.concatenate([v_ref[:, sls[h]] * bg[:, h:h + 1], kb[h] * egi[h]], axis=1))
           for h in hs]
    s = [s_scr[h] for h in hs]
    ws_qs = [_dot3(jnp.concatenate([sol[h][:, HEAD:], q[h] * egi[h]], axis=0), s[h]) for h in hs]
    v_new = [sol[h][:, :HEAD] - ws_qs[h][:c] for h in hs]
    o = [ws_qs[h][c:] + _dot3(qk[h], v_new[h]) for h in hs]
    for h in hs:
        o_ref[:, sls[h]] = o[h]
    g_last = [gi[h][c - 1:c, :] for h in hs]
    kd = [k[h] * jnp.exp(g_last[h] - gi[h]) for h in hs]
    s_new = [s[h] * jnp.exp(g_last[h]) + _dot3(kd[h], v_new[h], _dot_tn) for h in hs]
    for h in hs:
        s_scr[h] = s_new[h]

    @pl.when(n == pl.num_programs(0) - 1)
    def _():
        s_out_ref[...] = s_scr[...]


def _gdn_chunked(act, bg, s0, heads):
    t_len = act.shape[0]
    c = GDN_CHUNK
    assert t_len % c == 0
    width = heads * HEAD
    n_chunks = t_len // c
    g_t = bg[:, heads:].reshape(n_chunks, c, heads).transpose(0, 2, 1)
    return pl.pallas_call(
        functools.partial(_gdn_chunk_kernel, heads=heads, c=c),
        name="gdn_chunked",
        out_shape=(jax.ShapeDtypeStruct((t_len, width), F32),
                   jax.ShapeDtypeStruct((heads, HEAD, HEAD), F32)),
        grid=(n_chunks,),
        in_specs=[pl.BlockSpec((c, width), lambda n: (n, 0)),
                  pl.BlockSpec((c, width), lambda n: (n, 1)),
                  pl.BlockSpec((c, width), lambda n: (n, 2)),
                  pl.BlockSpec((c, 2 * heads), lambda n: (n, 0)),
                  pl.BlockSpec((1, heads, c), lambda n: (n, 0, 0)),
                  pl.BlockSpec((heads, HEAD, HEAD), lambda n: (0, 0, 0))],
        out_specs=(pl.BlockSpec((c, width), lambda n: (n, 0)),
                   pl.BlockSpec((heads, HEAD, HEAD), lambda n: (0, 0, 0))),
        scratch_shapes=[pltpu.VMEM((heads, HEAD, HEAD), F32)],
        compiler_params=_params("arbitrary"),
    )(act, act, act, bg, g_t, s0)


def _gdn_step_kernel(beta_ref, g_ref, qt_ref, kt_ref, v_ref, s0_ref, o_ref, s_ref, *, seqs, steps, heads):
    bo = pl.program_id(0)
    h = pl.program_id(1)
    for bl in range(seqs):
        s = s0_ref[bl]
        for t in range(steps):
            colx = bl * steps + t
            idx = ((bo * seqs + bl) * steps + t) * heads + h
            beta = beta_ref[idx]
            a = jnp.exp(jnp.full((1, HEAD), g_ref[idx], F32))
            kc = jnp.broadcast_to(kt_ref[:, colx:colx + 1], (HEAD, HEAD))
            qc = jnp.broadcast_to(qt_ref[:, colx:colx + 1], (HEAD, HEAD))
            r = jnp.sum(kc * s, axis=0, keepdims=True)
            u = beta * (v_ref[colx:colx + 1, :] - a * r)
            s = a * s + kc * u
            o_ref[colx:colx + 1, :] = jnp.sum(qc * s, axis=0, keepdims=True)
        s_ref[bl] = s


def _gdn_steps(act_tm, bg, s0, heads, seqs=8):
    steps, b, _ = act_tm.shape
    assert b % seqs == 0
    nbo = b // seqs
    width = heads * HEAD
    lanes = seqs * steps

    def cols(a):
        a = a.reshape(steps, nbo, seqs, heads, HEAD)
        return a.transpose(1, 3, 4, 2, 0).reshape(nbo, heads, HEAD, lanes)

    qt = cols(act_tm[..., :width])
    kt = cols(act_tm[..., width:2 * width])
    v = act_tm[..., 2 * width:].reshape(steps, nbo, seqs, heads, HEAD)
    v = v.transpose(1, 3, 2, 0, 4).reshape(nbo, heads, lanes, HEAD)
    beta = bg[:, :heads].reshape(-1)
    g = bg[:, heads:].reshape(-1)
    smem = pl.BlockSpec(memory_space=pltpu.SMEM)
    o, s_new = pl.pallas_call(
        functools.partial(_gdn_step_kernel, seqs=seqs, steps=steps, heads=heads),
        name="gdn_steps",
        out_shape=(jax.ShapeDtypeStruct((nbo, heads, lanes, HEAD), F32),
                   jax.ShapeDtypeStruct(s0.shape, F32)),
        grid=(nbo, heads),
        in_specs=[smem, smem,
                  pl.BlockSpec((None, None, HEAD, lanes), lambda i, h: (i, h, 0, 0)),
                  pl.BlockSpec((None, None, HEAD, lanes), lambda i, h: (i, h, 0, 0)),
                  pl.BlockSpec((None, None, lanes, HEAD), lambda i, h: (i, h, 0, 0)),
                  pl.BlockSpec((seqs, None, HEAD, HEAD), lambda i, h: (i, h, 0, 0))],
        out_specs=(pl.BlockSpec((None, None, lanes, HEAD), lambda i, h: (i, h, 0, 0)),
                   pl.BlockSpec((seqs, None, HEAD, HEAD), lambda i, h: (i, h, 0, 0))),
        compiler_params=_params("parallel", "parallel"),
    )(beta, g, qt, kt, v, s0)
    o = o.reshape(nbo, heads, seqs, steps, HEAD).transpose(0, 2, 3, 1, 4).reshape(b * steps, width)
    return o, s_new


def _gdn_out_gate(o, z, g):
    outs = []
    for h in range(o.shape[1] // HEAD):
        sl = slice(h * HEAD, (h + 1) * HEAD)
        outs.append(_rms(o[:, sl], g) * _silu(z[:, sl]))
    return jnp.concatenate(outs, axis=1)


def _mm_post_kernel(*refs, gated):
    if gated:
        a_ref, z_ref, ng_ref, w_ref, x_ref, gt_ref, pg_ref, o_ref = refs
        a = _gdn_out_gate(a_ref[...], z_ref[...], ng_ref[...]).astype(BF16)
    else:
        a_ref, w_ref, x_ref, gt_ref, pg_ref, o_ref = refs
        a = a_ref[...].astype(BF16)
    y = _dot(a, w_ref[...])
    o_ref[...] = x_ref[...] + gt_ref[...] * _rms(y, pg_ref[...])


def _matmul_post(a, w, x, gt, post_g, z=None, z_col=0, norm_g=None):
    m, k = a.shape
    d = w.shape[1]
    tm = _tile(m, 256)
    gated = z is not None
    in_specs = [pl.BlockSpec((tm, k), lambda i: (i, 0))]
    args = [a]
    if gated:
        in_specs += [pl.BlockSpec((tm, k), lambda i: (i, z_col)), pl.BlockSpec((1, HEAD), lambda i: (0, 0))]
        args += [z, norm_g.reshape(1, HEAD)]
    in_specs += [pl.BlockSpec((k, d), lambda i: (0, 0)),
                 pl.BlockSpec((tm, d), lambda i: (i, 0)),
                 _mod_spec(gt, tm, d),
                 pl.BlockSpec((1, d), lambda i: (0, 0))]
    args += [w, x, gt, post_g.reshape(1, d)]
    return pl.pallas_call(
        functools.partial(_mm_post_kernel, gated=gated),
        name="matmul_post",
        out_shape=jax.ShapeDtypeStruct((m, d), F32),
        grid=(m // tm,),
        in_specs=in_specs,
        out_specs=pl.BlockSpec((tm, d), lambda i: (i, 0)),
        compiler_params=_params("parallel"),
    )(*args)


def _ffn_kernel(x_ref, g_ref, sc_ref, sh_ref, wg_ref, wu_ref, wd_ref, gt_ref, pg_ref, o_ref, h_scr, acc):
    f = pl.program_id(1)

    @pl.when(f == 0)
    def _():
        h_scr[...] = _norm_mod(x_ref[...], g_ref[...], sc_ref[...], sh_ref[...]).astype(BF16)
        acc[...] = jnp.zeros_like(acc)

    h = h_scr[...]
    act = (_silu(_dot(h, wg_ref[...])) * _dot(h, wu_ref[...])).astype(BF16)
    acc[...] += _dot(act, wd_ref[...])

    @pl.when(f == pl.num_programs(1) - 1)
    def _():
        o_ref[...] = x_ref[...] + gt_ref[...] * _rms(acc[...], pg_ref[...])


def _dense_ffn(x, g, sc, sh, w_gu, w_down, gt, post_g, tm_pref=512, tf_pref=512):
    m, d = x.shape
    ff = w_down.shape[0]
    tm = _tile(m, tm_pref)
    tf = _tile(ff, tf_pref, LANES)
    nf = ff // tf
    return pl.pallas_call(
        _ffn_kernel,
        name="dense_ffn",
        out_shape=jax.ShapeDtypeStruct((m, d), F32),
        grid=(m // tm, nf),
        in_specs=[pl.BlockSpec((tm, d), lambda i, f: (i, 0)),
                  pl.BlockSpec((1, d), lambda i, f: (0, 0)),
                  _mod_spec(sc, tm, d), _mod_spec(sh, tm, d),
                  pl.BlockSpec((d, tf), lambda i, f: (0, f)),
                  pl.BlockSpec((d, tf), lambda i, f: (0, nf + f)),
                  pl.BlockSpec((tf, d), lambda i, f: (f, 0)),
                  _mod_spec(gt, tm, d),
                  pl.BlockSpec((1, d), lambda i, f: (0, 0))],
        out_specs=pl.BlockSpec((tm, d), lambda i, f: (i, 0)),
        scratch_shapes=[pltpu.VMEM((tm, d), BF16), pltpu.VMEM((tm, d), F32)],
        compiler_params=_params("parallel", "arbitrary"),
    )(x, g.reshape(1, d), sc, sh, w_gu, w_gu, w_down, gt, post_g.reshape(1, d))


def _router_kernel(x_ref, g_ref, sc_ref, sh_ref, wr_ref, h_ref, r_ref, *, experts):
    h = _norm_mod(x_ref[...], g_ref[...], sc_ref[...], sh_ref[...])
    h_ref[...] = h.astype(BF16)
    logits = _dot3(h, wr_ref[...])
    lane = lax.broadcasted_iota(jnp.int32, logits.shape, 1)
    lanef = lane.astype(F32)
    logits = jnp.where(lane < experts, logits, NEG_INF)
    m1 = jnp.max(logits, axis=-1, keepdims=True)
    i1 = jnp.min(jnp.where(logits == m1, lanef, float(LANES)), axis=-1, keepdims=True)
    rest = jnp.where(lanef == i1, NEG_INF, logits)
    m2 = jnp.max(rest, axis=-1, keepdims=True)
    i2 = jnp.min(jnp.where(rest == m2, lanef, float(LANES)), axis=-1, keepdims=True)
    e2 = jnp.exp(m2 - m1)
    w1 = 1.0 / (1.0 + e2)
    w2 = e2 / (1.0 + e2)
    r_ref[...] = jnp.where(lane == 0, i1, jnp.where(lane == 1, i2, jnp.where(lane == 2, w1, jnp.where(lane == 3, w2, 0.0))))


def _router(x, g, sc, sh, w_router):
    m, d = x.shape
    experts = w_router.shape[1]
    tm = _tile(m, 512)
    wr = jnp.pad(w_router, ((0, 0), (0, LANES - experts)))
    return pl.pallas_call(
        functools.partial(_router_kernel, experts=experts),
        name="moe_router",
        out_shape=(jax.ShapeDtypeStruct((m, d), BF16), jax.ShapeDtypeStruct((m, LANES), F32)),
        grid=(m // tm,),
        in_specs=[pl.BlockSpec((tm, d), lambda i: (i, 0)),
                  pl.BlockSpec((1, d), lambda i: (0, 0)),
                  _mod_spec(sc, tm, d), _mod_spec(sh, tm, d),
                  pl.BlockSpec((d, LANES), lambda i: (0, 0))],
        out_specs=(pl.BlockSpec((tm, d), lambda i: (i, 0)), pl.BlockSpec((tm, LANES), lambda i: (i, 0))),
        compiler_params=_params("parallel"),
    )(x, g.reshape(1, d), sc, sh, wr)


def _moe_ffn_kernel(te_ref, nu_ref, h_ref, wg_ref, wu_ref, wd_ref, o_ref, acc):
    i = pl.program_id(0)
    f = pl.program_id(1)
    last = pl.num_programs(1) - 1
    used = i < nu_ref[0]

    @pl.when(jnp.logical_and(used, f == 0))
    def _():
        acc[...] = jnp.zeros_like(acc)

    @pl.when(used)
    def _():
        h = h_ref[...]
        act = (_silu(_dot(h, wg_ref[...].astype(BF16))) * _dot(h, wu_ref[...].astype(BF16))).astype(BF16)
        acc[...] += _dot(act, wd_ref[...].astype(BF16))

    @pl.when(jnp.logical_and(used, f == last))
    def _():
        o_ref[...] = acc[...]

    @pl.when(jnp.logical_and(jnp.logical_not(used), f == last))
    def _():
        o_ref[...] = jnp.zeros_like(o_ref)


def _moe_ffn(hs, tile_expert, n_used, w_gu, w_down, tm, tf_pref=512):
    mp, d = hs.shape
    ff = w_down.shape[1]
    tf = _tile(ff, tf_pref, LANES)
    nf = ff // tf

    def fidx(i, f, nu):
        return jnp.where(i < nu[0], f, nf - 1)

    grid_spec = pltpu.PrefetchScalarGridSpec(
        num_scalar_prefetch=2,
        grid=(mp // tm, nf),
        in_specs=[pl.BlockSpec((tm, d), lambda i, f, te, nu: (i, 0), pipeline_mode=pl.Buffered(1)),
                  pl.BlockSpec((None, d, tf), lambda i, f, te, nu: (te[i], 0, fidx(i, f, nu))),
                  pl.BlockSpec((None, d, tf), lambda i, f, te, nu: (te[i], 0, nf + fidx(i, f, nu))),
                  pl.BlockSpec((None, tf, d), lambda i, f, te, nu: (te[i], fidx(i, f, nu), 0))],
        out_specs=pl.BlockSpec((tm, d), lambda i, f, te, nu: (i, 0), pipeline_mode=pl.Buffered(1)),
        scratch_shapes=[pltpu.VMEM((tm, d), F32)],
    )
    return pl.pallas_call(
        _moe_ffn_kernel,
        name="moe_ffn",
        out_shape=jax.ShapeDtypeStruct((mp, d), F32),
        grid_spec=grid_spec,
        compiler_params=_params("arbitrary", "arbitrary"),
    )(tile_expert, n_used, hs, w_gu, w_gu, w_down)


def _combine_kernel(y1_ref, y2_ref, r_ref, x_ref, gt_ref, pg_ref, o_ref):
    r = r_ref[...]
    y = r[:, 2:3] * y1_ref[...] + r[:, 3:4] * y2_ref[...]
    o_ref[...] = x_ref[...] + gt_ref[...] * _rms(y, pg_ref[...])


def _moe_combine(y1, y2, route, x, gt, post_g):
    m, d = x.shape
    tm = _tile(m, 256)
    return pl.pallas_call(
        _combine_kernel,
        name="moe_combine",
        out_shape=jax.ShapeDtypeStruct((m, d), F32),
        grid=(m // tm,),
        in_specs=[pl.BlockSpec((tm, d), lambda i: (i, 0)),
                  pl.BlockSpec((tm, d), lambda i: (i, 0)),
                  pl.BlockSpec((tm, LANES), lambda i: (i, 0)),
                  pl.BlockSpec((tm, d), lambda i: (i, 0)),
                  _mod_spec(gt, tm, d),
                  pl.BlockSpec((1, d), lambda i: (0, 0))],
        out_specs=pl.BlockSpec((tm, d), lambda i: (i, 0)),
        compiler_params=_params("parallel"),
    )(y1, y2, route, x, gt, post_g.reshape(1, d))


def _moe_layer(parts, g, post_g, w_router, w_gu, w_down, tm=768):
    experts = w_router.shape[1]
    routed = [_router(x, g, sc, sh, w_router) for x, sc, sh, _ in parts]
    h = jnp.concatenate([r[0] for r in routed], axis=0)
    route = jnp.concatenate([r[1] for r in routed], axis=0)
    m = h.shape[0]
    ids = route[:, :MOE_TOPK].astype(jnp.int32).reshape(-1)
    n_assign = m * MOE_TOPK
    hot = (ids[:, None] == jnp.arange(experts, dtype=jnp.int32)[None, :]).astype(jnp.int32)
    seen = jnp.cumsum(hot, axis=0)
    counts = seen[-1]
    padded = ((counts + tm - 1) // tm) * tm
    starts_p = jnp.cumsum(padded) - padded
    dest = jnp.sum(hot * (starts_p[None, :] + seen - 1), axis=1).astype(jnp.int32)
    n_tiles = -(-(n_assign + experts * (tm - 1)) // tm)
    mp = n_tiles * tm
    src_tok = jnp.zeros((mp,), jnp.int32).at[dest].set(jnp.arange(n_assign, dtype=jnp.int32) // MOE_TOPK)
    tile_start = jnp.arange(n_tiles, dtype=jnp.int32) * tm
    ends_p = jnp.cumsum(padded)
    tile_expert = jnp.sum((ends_p[None, :] <= tile_start[:, None]).astype(jnp.int32), axis=1)
    tile_expert = jnp.minimum(tile_expert, experts - 1)
    n_used = (ends_p[-1] // tm).astype(jnp.int32).reshape(1)
    last_e = tile_expert[jnp.maximum(n_used[0] - 1, 0)]
    tile_expert = jnp.where(jnp.arange(n_tiles) < n_used[0], tile_expert, last_e)
    hs = jnp.take(h, src_tok, axis=0, mode="clip")
    ys = _moe_ffn(hs, tile_expert, n_used, w_gu, w_down, tm)
    dest = dest.reshape(m, MOE_TOPK)
    outs = []
    row0 = 0
    for (x, _, _, gt), (_, route_x) in zip(parts, routed):
        rows = slice(row0, row0 + x.shape[0])
        row0 += x.shape[0]
        y1 = jnp.take(ys, dest[rows, 0], axis=0, mode="clip")
        y2 = jnp.take(ys, dest[rows, 1], axis=0, mode="clip")
        outs.append(_moe_combine(y1, y2, route_x, x, gt, post_g))
    return outs


def _block_mean_kernel(k_ref, o_ref):
    o_ref[0] = jnp.mean(k_ref[...], axis=0, keepdims=True)


def _block_means(k):
    t_len, width = k.shape
    nb = t_len // MOBA_BLOCK
    out = pl.pallas_call(
        _block_mean_kernel,
        name="moba_block_means",
        out_shape=jax.ShapeDtypeStruct((nb, 1, width), F32),
        grid=(nb,),
        in_specs=[pl.BlockSpec((MOBA_BLOCK, width), lambda j: (j, 0))],
        out_specs=pl.BlockSpec((1, 1, width), lambda j: (j, 0, 0)),
        compiler_params=_params("parallel"),
    )(k)
    return out.reshape(nb, width)


def _moba_seq_kernel(q_ref, k_ref, vt_ref, km_ref, o_ref, *, nb, hb):
    bs = MOBA_BLOCK
    ob = pl.program_id(1)
    own = pl.multiple_of(ob * bs, bs)
    causal = lax.broadcasted_iota(jnp.int32, (bs, bs), 0) <= lax.broadcasted_iota(jnp.int32, (bs, bs), 1)
    blk = lax.broadcasted_iota(jnp.int32, (bs, LANES), 1)
    blk_t = lax.broadcasted_iota(jnp.int32, (LANES, bs), 0)
    blkf_t = blk_t.astype(F32)
    hsl = [slice(h * HEAD, (h + 1) * HEAD) for h in range(hb)]

    qts = []
    for h in range(hb):
        q = q_ref[:, hsl[h]]
        gate = _dot3(km_ref[:, hsl[h]], q, _dot_nt)
        gate = jnp.where(blk_t < ob, gate, NEG_INF)
        sel = jnp.zeros((LANES, bs), F32)
        for _ in range(MOBA_TOPK):
            m = jnp.max(gate, axis=0, keepdims=True)
            first = jnp.min(jnp.where(gate == m, blkf_t, float(LANES)), axis=0, keepdims=True)
            pick = jnp.logical_and(blkf_t == first, m > NEG_INF)
            sel = jnp.where(pick, 1.0, sel)
            gate = jnp.where(pick, NEG_INF, gate)
        bias_t = jnp.where(sel > 0.0, 0.0, MASKED)
        q_t = (q * (HEAD ** -0.5 * math.log2(math.e))).T
        qts.append(jnp.concatenate([q_t, bias_t], axis=0).astype(BF16))

    def scores(h, start, hot):
        k_ext = jnp.concatenate([k_ref[pl.ds(start, bs), hsl[h]], hot], axis=1)
        return _dot(k_ext, qts[h])

    def one_hot(j):
        return jnp.where(blk == j, 1.0, 0.0).astype(BF16)

    no_hot = jnp.zeros((bs, LANES), BF16)
    ms, ls, accs = [], [], []
    for h in range(hb):
        s = jnp.where(causal, scores(h, own, no_hot), MASKED)
        m = jnp.max(s, axis=0, keepdims=True)
        p = jnp.exp2(s - m)
        ms.append(m)
        ls.append(jnp.sum(p, axis=0, keepdims=True))
        accs.append(_dot(vt_ref[ob, hsl[h], :], p.astype(BF16)))

    def body(j, carry):
        ss, ms, ls, accs = carry
        jn = jnp.minimum(j + 1, nb - 1)
        s_next = tuple(scores(h, pl.multiple_of(jn * bs, bs), one_hot(jn)) for h in range(hb))
        m_out, l_out, a_out = [], [], []
        for h in range(hb):
            m_new = jnp.maximum(ms[h], jnp.max(ss[h], axis=0, keepdims=True))
            alpha = jnp.exp2(ms[h] - m_new)
            p = jnp.exp2(ss[h] - m_new)
            m_out.append(m_new)
            l_out.append(alpha * ls[h] + jnp.sum(p, axis=0, keepdims=True))
            a_out.append(alpha * accs[h] + _dot(vt_ref[j, hsl[h], :], p.astype(BF16)))
        return s_next, tuple(m_out), tuple(l_out), tuple(a_out)

    s_first = tuple(scores(h, 0, one_hot(0)) for h in range(hb))
    _, _, ls, accs = lax.fori_loop(0, ob, body, (s_first, tuple(ms), tuple(ls), tuple(accs)))
    for h in range(hb):
        o_ref[:, hsl[h]] = (accs[h] / ls[h]).T.astype(o_ref.dtype)


def _moba_seq(q, k, kb, vt, heads, hb=4):
    t_len, width = q.shape
    bs = MOBA_BLOCK
    assert t_len % bs == 0 and heads % hb == 0
    nb = t_len // bs
    assert nb <= LANES
    kmean = jnp.pad(_block_means(k), ((0, LANES - nb), (0, 0)))
    return pl.pallas_call(
        functools.partial(_moba_seq_kernel, nb=nb, hb=hb),
        name="moba_seq",
        out_shape=jax.ShapeDtypeStruct((t_len, width), BF16),
        grid=(heads // hb, nb),
        in_specs=[pl.BlockSpec((bs, hb * HEAD), lambda h, i: (i, h)),
                  pl.BlockSpec((t_len, hb * HEAD), lambda h, i: (0, h)),
                  pl.BlockSpec((nb, hb * HEAD, bs), lambda h, i: (0, h, 0)),
                  pl.BlockSpec((LANES, hb * HEAD), lambda h, i: (0, h))],
        out_specs=pl.BlockSpec((bs, hb * HEAD), lambda h, i: (i, h)),
        compiler_params=_params("parallel", "arbitrary"),
    )(q, kb, vt, kmean)


def _head_match_bias(n_rows, heads, cols_used):
    row = lax.broadcasted_iota(jnp.int32, (n_rows, LANES), 0)
    col = lax.broadcasted_iota(jnp.int32, (n_rows, LANES), 1)
    keep = jnp.logical_and(row % heads == col % heads, col < cols_used)
    return keep, row, col


def _moba_paged_kernel(pt_ref, qt_ref, kn_ref, vn_ref, *refs, heads, steps, nblk, page, bpb):
    k_refs = refs[:2 * bpb]
    v_refs = refs[2 * bpb:4 * bpb]
    o_ref, qtb, bias, acc, ks, st_m, st_l, st_w = refs[4 * bpb:]
    j = pl.program_id(1)
    cols = steps * heads
    n_rows = page * heads

    @pl.when(j == 0)
    def _():
        qtb[...] = (qt_ref[0] * (HEAD ** -0.5 * math.log2(math.e))).astype(BF16)
        keep, _, _ = _head_match_bias(n_rows, heads, cols)
        bias[...] = jnp.where(keep, 0.0, MASKED)

    pages = range(2 * bpb)
    s = [_dot(k_refs[p][0].astype(BF16), qtb[...]) + bias[...] for p in pages]
    pmax = [jnp.max(s[p], axis=0, keepdims=True) for p in pages]
    m = [jnp.maximum(pmax[2 * u], pmax[2 * u + 1]) for u in range(bpb)]
    prob = [jnp.exp2(s[p] - m[p // 2]) for p in pages]
    psum = [jnp.sum(prob[p], axis=0, keepdims=True) for p in pages]
    pv = [_dot(prob[p].T.astype(BF16), v_refs[p][0].astype(BF16)) for p in pages]
    ksum = [jnp.sum(k_refs[p][0].reshape(page, heads, HEAD), axis=0) for p in pages]
    for u in range(bpb):
        blk = j * bpb + u
        acc[blk] = pv[2 * u] + pv[2 * u + 1]
        ks[blk] = ksum[2 * u] + ksum[2 * u + 1]
        st_m[pl.ds(blk, 1), :] = m[u]
        st_l[pl.ds(blk, 1), :] = psum[2 * u] + psum[2 * u + 1]

    @pl.when(j == nblk // bpb - 1)
    def _():
        qt = qt_ref[0]
        hrow = lax.broadcasted_iota(jnp.int32, (heads, LANES), 0)
        hcol = lax.broadcasted_iota(jnp.int32, (heads, LANES), 1) % heads
        gates = []
        for b in range(nblk):
            g_all = _dot3(ks[b], qt) * (1.0 / (2 * page))
            gates.append(jnp.sum(jnp.where(hrow == hcol, g_all, 0.0), axis=0, keepdims=True))
        sel = []
        for b in range(nblk):
            rank = jnp.zeros((1, LANES), F32)
            for o in range(nblk):
                if o == b:
                    continue
                ahead = gates[o] > gates[b]
                if o < b:
                    ahead = jnp.logical_or(ahead, gates[o] == gates[b])
                rank = rank + jnp.where(ahead, 1.0, 0.0)
            sel.append(rank < float(MOBA_TOPK))
        keep, row, col = _head_match_bias(cols, heads, cols)
        keep = jnp.logical_and(keep, row // heads <= col // heads)
        s_own = _dot(kn_ref[0].astype(BF16), qtb[...]) + jnp.where(keep, 0.0, MASKED)
        m_own = jnp.max(s_own, axis=0, keepdims=True)
        p_own = jnp.exp2(s_own - m_own)
        l_own = jnp.sum(p_own, axis=0, keepdims=True)
        v_own = vn_ref[0]
        if cols < LANES:
            p_own = jnp.concatenate([p_own, jnp.zeros((LANES - cols, LANES), F32)], axis=0)
            v_own = jnp.concatenate([v_own, jnp.zeros((LANES - cols, HEAD), F32)], axis=0)
        acc_own = _dot(p_own.T.astype(BF16), v_own.astype(BF16))
        m_all = m_own
        for b in range(nblk):
            m_all = jnp.maximum(m_all, jnp.where(sel[b], st_m[b:b + 1, :], MASKED))
        w_own = jnp.exp2(m_own - m_all)
        den = w_own * l_own
        ws = []
        for b in range(nblk):
            wb = jnp.where(sel[b], jnp.exp2(st_m[b:b + 1, :] - m_all), 0.0)
            den = den + wb * st_l[b:b + 1, :]
            ws.append(wb)
        inv = 1.0 / den
        st_w[...] = jnp.zeros_like(st_w)
        for b in range(nblk):
            st_w[b:b + 1, :] = ws[b] * inv
        st_w[nblk:nblk + 1, :] = w_own * inv
        eye_r = lax.broadcasted_iota(jnp.int32, (LANES, LANES), 0)
        eye_c = lax.broadcasted_iota(jnp.int32, (LANES, LANES), 1)
        eye = (eye_r == eye_c).astype(BF16)
        wr = st_w[...]
        w_hi, w_mid = _split(wr)
        w_lo = (wr - w_hi.astype(F32) - w_mid.astype(F32)).astype(BF16)
        wcol = _dot_nt(eye, w_hi) + (_dot_nt(eye, w_mid) + _dot_nt(eye, w_lo))
        out = wcol[:, nblk:nblk + 1] * acc_own
        for b in range(nblk):
            out = out + wcol[:, b:b + 1] * acc[b]
        o_ref[0] = out[:cols, :]


def _moba_paged(q, k_new, v_new, cache_k, cache_v, page_table, heads):
    b, steps, width = q.shape
    n_phys, page = cache_k.shape[0], cache_k.shape[1]
    n_pages = page_table.shape[1]
    assert MOBA_BLOCK == 2 * page and n_pages % 2 == 0 and steps <= MOBA_BLOCK
    nblk = n_pages // 2
    cols = steps * heads
    n_rows = page * heads
    stat_rows = -(-(nblk + 1) // 8) * 8
    assert cols <= LANES
    ck = cache_k.reshape(n_phys, n_rows, HEAD)
    cv = cache_v.reshape(n_phys, n_rows, HEAD)
    pt = page_table.reshape(-1).astype(jnp.int32)
    qt = jnp.pad(q.reshape(b, cols, HEAD).transpose(0, 2, 1), ((0, 0), (0, 0), (0, LANES - cols)))
    kn = k_new.reshape(b, cols, HEAD)
    vn = v_new.reshape(b, cols, HEAD)

    bpb = 4 if nblk % 4 == 0 else (2 if nblk % 2 == 0 else 1)
    ppb = 2 * bpb

    def page_spec(off):
        return pl.BlockSpec((1, n_rows, HEAD), lambda i, j, pt: (pt[i * n_pages + ppb * j + off], 0, 0))

    page_specs = [page_spec(off) for off in range(ppb)]
    new_spec = pl.BlockSpec((1, cols, HEAD), lambda i, j, pt: (i, 0, 0))
    grid_spec = pltpu.PrefetchScalarGridSpec(
        num_scalar_prefetch=1,
        grid=(b, nblk // bpb),
        in_specs=[pl.BlockSpec((1, HEAD, LANES), lambda i, j, pt: (i, 0, 0)), new_spec, new_spec]
        + page_specs + page_specs,
        out_specs=pl.BlockSpec((1, cols, HEAD), lambda i, j, pt: (i, 0, 0)),
        scratch_shapes=[pltpu.VMEM((HEAD, LANES), BF16), pltpu.VMEM((n_rows, LANES), F32),
                        pltpu.VMEM((nblk, LANES, HEAD), F32), pltpu.VMEM((nblk, heads, HEAD), F32),
                        pltpu.VMEM((stat_rows, LANES), F32), pltpu.VMEM((stat_rows, LANES), F32),
                        pltpu.VMEM((stat_rows, LANES), F32)],
    )
    out = pl.pallas_call(
        functools.partial(_moba_paged_kernel, heads=heads, steps=steps, nblk=nblk, page=page, bpb=bpb),
        name="moba_paged",
        out_shape=jax.ShapeDtypeStruct((b, cols, HEAD), F32),
        grid_spec=grid_spec,
        compiler_params=_params("parallel", "arbitrary"),
    )(pt, qt, kn, vn, *([ck] * ppb), *([cv] * ppb))
    return out.reshape(b, steps, width)


def _rope_tables(pos):
    half = HEAD // 2
    inv = ROPE_THETA ** (-jnp.arange(half, dtype=F32) / half)
    ang = pos.astype(F32)[:, None] * inv[None, :]
    cos = jnp.cos(ang)
    sin = jnp.sin(ang)
    return jnp.concatenate([cos, cos], axis=1), jnp.concatenate([-sin, sin], axis=1)


def _chunks(mod, rep):
    parts = jnp.split(mod, 6, axis=-1)
    return [jnp.repeat(p, rep, axis=0) if rep > 1 else p for p in parts]


def kernel(x_prompt, x_sample, cache_k, cache_v, page_table, state_delta, state_conv, c_prompt, c_sample, ada_w, ada_b, ln_mix_pre, ln_mix_post, ln_ffn_pre, ln_ffn_post, a_w_in, a_conv_w, a_A_log, a_dt_bias, a_norm_g, a_w_out, kv_ada_w, kv_ada_b, kv_norm_g, w_kv, b_w_q, b_w_out, f_w_gu, f_w_down, m_router, m_w_gu, m_w_down):
    bp, t_p, d = x_prompt.shape
    bs, t_s, _ = x_sample.shape
    assert bp == 1
    gh = a_A_log.shape[1]
    ah = cache_k.shape[2]
    gw = gh * HEAD
    aw = ah * HEAD
    past = page_table.shape[1] * cache_k.shape[1]

    c_all = jnp.concatenate([c_prompt, c_sample], axis=0)
    r_all = c_all.shape[0]
    c_all = jnp.pad(c_all, ((0, (-r_all) % 8), (0, 0)))
    mods = [_modulation(c_all, ada_w, ada_b, l) for l in range(2)]
    kv_mod = _modulation(c_all, kv_ada_w[None], kv_ada_b[None])

    def mod_p(m):
        return m[0:1]

    def mod_s(m):
        return m[1:1 + bs]

    xp = x_prompt.reshape(t_p, d)
    xs = x_sample.reshape(bs * t_s, d)

    w_in = a_w_in[0]
    n_conv = 3 * gw
    w_ba = w_in[:, n_conv + gw:]
    w_out0 = a_w_out[0].astype(BF16)

    sh_m, sc_m, gt_m, sh_f, sc_f, gt_f = _chunks(mod_p(mods[0]), 1)
    proj_p = _norm_mod_matmul(xp, ln_mix_pre[0], sc_m, sh_m, w_in, n=n_conv + gw)
    bg_p = _gdn_gates(xp, ln_mix_pre[0], sc_m, sh_m, w_ba, a_A_log[0], a_dt_bias[0])
    act_p = _gdn_conv_seq(proj_p, a_conv_w[0], jnp.zeros((CONV_W - 1, n_conv), F32), gw)
    o_p, delta_p = _gdn_chunked(act_p, bg_p, jnp.zeros((gh, HEAD, HEAD), F32), gh)
    conv_p = proj_p[t_p - (CONV_W - 1):, :n_conv]
    xp = _matmul_post(o_p, w_out0, xp, gt_m, ln_mix_post[0], z=proj_p, z_col=3, norm_g=a_norm_g[0])

    ssh_m, ssc_m, sgt_m, ssh_f, ssc_f, sgt_f = _chunks(mod_s(mods[0]), t_s)
    proj_s = _norm_mod_matmul(xs, ln_mix_pre[0], ssc_m, ssh_m, w_in, n=n_conv + gw)
    bg_s = _gdn_gates(xs, ln_mix_pre[0], ssc_m, ssh_m, w_ba, a_A_log[0], a_dt_bias[0])
    qkv_s = proj_s[:, :n_conv].reshape(bs, t_s, n_conv)
    xc_s = jnp.concatenate([state_conv[:, 0].transpose(1, 0, 2), qkv_s.transpose(1, 0, 2)], axis=0)
    act_s = _gdn_conv_steps(xc_s, a_conv_w[0], gw)
    o_s, delta_s = _gdn_steps(act_s, bg_s, state_delta[:, 0], gh)
    conv_s = qkv_s[:, t_s - (CONV_W - 1):]
    xs = _matmul_post(o_s, w_out0, xs, sgt_m, ln_mix_post[0], z=proj_s, z_col=3, norm_g=a_norm_g[0])

    w_gu0 = f_w_gu[0].astype(BF16)
    w_dn0 = f_w_down[0].astype(BF16)
    xp = _dense_ffn(xp, ln_ffn_pre[0], sc_f, sh_f, w_gu0, w_dn0, gt_f, ln_ffn_post[0])
    xs = _dense_ffn(xs, ln_ffn_pre[0], ssc_f, ssh_f, w_gu0, w_dn0, sgt_f, ln_ffn_post[0])

    cos_p, sin_p = _rope_tables(jnp.arange(t_p))
    cos_s, sin_s = _rope_tables(jnp.tile(past + jnp.arange(t_s), bs))
    ksh, ksc = jnp.split(kv_mod, 2, axis=-1)
    ksc_s = jnp.repeat(mod_s(ksc), t_s, axis=0)
    ksh_s = jnp.repeat(mod_s(ksh), t_s, axis=0)
    k_p, kb_p = _norm_mod_matmul(xp, kv_norm_g, mod_p(ksc), mod_p(ksh), w_kv, 0, aw, cos_p, sin_p, rope=True,
                                 copy="bf16")
    v_p, vt_p = _norm_mod_matmul(xp, kv_norm_g, mod_p(ksc), mod_p(ksh), w_kv, aw, aw, copy="bf16_blocks_t")
    k_s = _norm_mod_matmul(xs, kv_norm_g, ksc_s, ksh_s, w_kv, 0, aw, cos_s, sin_s, rope=True)
    v_s = _norm_mod_matmul(xs, kv_norm_g, ksc_s, ksh_s, w_kv, aw, aw)

    w_out1 = b_w_out[0].astype(BF16)
    sh_m, sc_m, gt_m, sh_f, sc_f, gt_f = _chunks(mod_p(mods[1]), 1)
    ssh_m, ssc_m, sgt_m, ssh_f, ssc_f, sgt_f = _chunks(mod_s(mods[1]), t_s)
    q_p = _norm_mod_matmul(xp, ln_mix_pre[1], sc_m, sh_m, b_w_q[0], cos=cos_p, sin=sin_p, rope=True)
    q_s = _norm_mod_matmul(xs, ln_mix_pre[1], ssc_m, ssh_m, b_w_q[0], cos=cos_s, sin=sin_s, rope=True)
    att_p = _moba_seq(q_p, k_p, kb_p, vt_p, ah)
    att_s = _moba_paged(q_s.reshape(bs, t_s, aw), k_s.reshape(bs, t_s, aw), v_s.reshape(bs, t_s, aw),
                        cache_k, cache_v, page_table, ah)
    xp = _matmul_post(att_p, w_out1, xp, gt_m, ln_mix_post[1])
    xs = _matmul_post(att_s.reshape(bs * t_s, aw), w_out1, xs, sgt_m, ln_mix_post[1])

    y_p, y_s = _moe_layer([(xp, sc_f, sh_f, gt_f), (xs, ssc_f, ssh_f, sgt_f)], ln_ffn_pre[1], ln_ffn_post[1],
                          m_router[0], m_w_gu[0], m_w_down[0])

    return (y_p.reshape(1, t_p, d), y_s.reshape(bs, t_s, d),
            delta_p.reshape(1, 1, gh, HEAD, HEAD), conv_p.reshape(1, 1, CONV_W - 1, n_conv),
            k_p.reshape(1, t_p, ah, HEAD), v_p.reshape(1, t_p, ah, HEAD),
            delta_s.reshape(bs, 1, gh, HEAD, HEAD), conv_s.reshape(bs, 1, CONV_W - 1, n_conv),
            k_s.reshape(bs, t_s, ah, HEAD), v_s.reshape(bs, t_s, ah, HEAD))
```
